```python
import jax, jax.numpy as jnp
from jax import lax
import numpy as np

D_MODEL = 1024
BATCH = 16
SEQ = 4096
DEPTH = 2
DEC_BATCH = 32
DEC_SEQ = 32
PAST_LEN = 2048

CHUNK = 64
WINDOW = 128
WIN_CHUNKS = WINDOW // CHUNK
N_HEADS = 16
N_KV_HEADS = 2
GROUP = N_HEADS // N_KV_HEADS
HEAD_DIM = 64
ATTN_WIDTH = N_HEADS * HEAD_DIM
KV_WIDTH = N_KV_HEADS * HEAD_DIM
POOL_WINDOWS = (2, 4, 8, 16)
POOL_GROUPS = len(POOL_WINDOWS)
POOL_GROUP_DIM = 128
POOL_WIDTH = POOL_GROUPS * POOL_GROUP_DIM
POOL_HIST = max(POOL_WINDOWS) - 1
GATE_WIDTH = 2 * D_MODEL
IN_WIDTH = ATTN_WIDTH + 2 * KV_WIDTH + POOL_WIDTH + GATE_WIDTH
D_FF = 4096
N_EXPERTS = 8
TOP_K = 2
EXPERT_FF = 3584
PLE_DIM = 256
N_DENSE = (DEPTH + 1) // 2
N_MOE = DEPTH // 2
EPS = 1e-6
NEG_INF = -1e30

kernel_name = "hybrid_swa_sink_pool_moe_stream_step"


def rms_norm(x, g):
    xf = x.astype(jnp.float32)
    y = xf * lax.rsqrt(jnp.mean(xf * xf, axis=-1, keepdims=True) + EPS)
    return (y * g.astype(jnp.float32)).astype(x.dtype)


def alibi_slopes():
    return 2.0 ** (-8.0 * jnp.arange(1, N_HEADS + 1, dtype=jnp.float32) / N_HEADS)


def banded_attention(qb, kb, vb, qpos, kpos, kvalid, sinks):
    s = jnp.einsum('bnqkgd,bnskd->bnkgqs', qb, kb,
                   preferred_element_type=jnp.float32) * (HEAD_DIM ** -0.5)
    dist = jnp.abs(qpos[:, :, None] - kpos[:, None, :]).astype(jnp.float32)
    slopes = alibi_slopes().reshape(N_KV_HEADS, GROUP)
    s = s - slopes[None, None, :, :, None, None] * dist[None, :, None, None, :, :]
    s = jnp.where(kvalid[None, :, None, None, None, :], s, NEG_INF)
    sink = sinks.astype(jnp.float32).reshape(N_KV_HEADS, GROUP)[None, None, :, :, None]
    m = jnp.maximum(jnp.max(s, axis=-1), sink)
    e = jnp.exp(s - m[..., None])
    denom = jnp.sum(e, axis=-1) + jnp.exp(sink - m)
    prob = (e / denom[..., None]).astype(vb.dtype)
    return jnp.einsum('bnkgqs,bnskd->bnqkgd', prob, vb)


def attention_prompt(q, k, v, sinks):
    B, T = q.shape[:2]
    n = T // CHUNK
    qb = q.reshape(B, n, CHUNK, N_KV_HEADS, GROUP, HEAD_DIM)
    kc = k.reshape(B, n, CHUNK, N_KV_HEADS, HEAD_DIM)
    vc = v.reshape(B, n, CHUNK, N_KV_HEADS, HEAD_DIM)
    pad = ((0, 0), (WIN_CHUNKS, 0), (0, 0), (0, 0), (0, 0))
    kp = jnp.pad(kc, pad)
    vp = jnp.pad(vc, pad)
    kb = jnp.concatenate([kp[:, j:j + n] for j in range(WIN_CHUNKS + 1)], axis=2)
    vb = jnp.concatenate([vp[:, j:j + n] for j in range(WIN_CHUNKS + 1)], axis=2)
    chunk = jnp.arange(n)
    qpos = chunk[:, None] * CHUNK + jnp.arange(CHUNK)[None, :]
    kchunk = chunk[:, None] - WIN_CHUNKS + jnp.arange(WIN_CHUNKS + 1)[None, :]
    kpos = (kchunk[:, :, None] * CHUNK + jnp.arange(CHUNK)[None, None, :]).reshape(n, -1)
    kvalid = jnp.repeat(kchunk >= 0, CHUNK, axis=1)
    o = banded_attention(qb, kb, vb, qpos, kpos, kvalid, sinks)
    return o.reshape(B, T, ATTN_WIDTH)


def attention_sample(q, k_all, v_all, sinks):
    B, S = q.shape[:2]
    qb = q.reshape(B, 1, S, N_KV_HEADS, GROUP, HEAD_DIM)
    qpos = (PAST_LEN + jnp.arange(S))[None, :]
    kpos = (PAST_LEN - WINDOW + jnp.arange(WINDOW + S))[None, :]
    kvalid = jnp.ones((1, WINDOW + S), dtype=bool)
    o = banded_attention(qb, k_all[:, None], v_all[:, None], qpos, kpos, kvalid, sinks)
    return o.reshape(B, S, ATTN_WIDTH)


def multiscale_pool(u, u_hist, pos, w_pg, scale):
    B, T = u.shape[:2]
    ext = jnp.concatenate([u_hist, u], axis=1).astype(jnp.float32)
    cs = jnp.pad(jnp.cumsum(ext, axis=1), ((0, 0), (1, 0), (0, 0)))
    outs = []
    for g, w in enumerate(POOL_WINDOWS):
        lo, hi = g * POOL_GROUP_DIM, (g + 1) * POOL_GROUP_DIM
        end = cs[:, POOL_HIST + 1:POOL_HIST + 1 + T, lo:hi]
        start = cs[:, POOL_HIST + 1 - w:POOL_HIST + 1 - w + T, lo:hi]
        cnt = jnp.minimum(pos + 1, w).astype(jnp.float32)[None, :, None]
        outs.append((end - start) / cnt)
    pooled = jnp.concatenate(outs, axis=-1)
    mixed = (pooled - u.astype(jnp.float32)).astype(u.dtype)
    mixed = mixed.reshape(B, T, POOL_GROUPS, POOL_GROUP_DIM)
    y = jnp.einsum('btgc,gcd->btgd', mixed, w_pg).reshape(B, T, POOL_WIDTH)
    return y * scale


def token_mixer(h, pos, hist, w_in, b_in, sinks, w_attn_up, w_pg, pool_scale,
                w_pool_up, w_out, g_pre, g_post):
    B, T = h.shape[:2]
    xn = rms_norm(h, g_pre)
    z = xn @ w_in + b_in
    c1 = ATTN_WIDTH
    c2 = c1 + KV_WIDTH
    c3 = c2 + KV_WIDTH
    c4 = c3 + POOL_WIDTH
    q = z[..., :c1]
    k = z[..., c1:c2].reshape(B, T, N_KV_HEADS, HEAD_DIM)
    v = z[..., c2:c3].reshape(B, T, N_KV_HEADS, HEAD_DIM)
    u = z[..., c3:c4]
    gates = jax.nn.sigmoid(z[..., c4:])
    if hist is None:
        attn = attention_prompt(q, k, v, sinks)
        k_state = k[:, -WINDOW:]
        v_state = v[:, -WINDOW:]
        u_hist = jnp.zeros((B, POOL_HIST, POOL_WIDTH), u.dtype)
        u_state = u[:, -POOL_HIST:]
    else:
        k_hist, v_hist, u_hist = hist
        k_all = jnp.concatenate([k_hist, k], axis=1)
        v_all = jnp.concatenate([v_hist, v], axis=1)
        attn = attention_sample(q, k_all, v_all, sinks)
        k_state = k_all[:, -WINDOW:]
        v_state = v_all[:, -WINDOW:]
        u_state = jnp.concatenate([u_hist, u], axis=1)[:, -POOL_HIST:]
    pool = multiscale_pool(u, u_hist, pos, w_pg, pool_scale)
    g_a = gates[..., :D_MODEL]
    g_p = gates[..., D_MODEL:]
    merged = g_a * (attn @ w_attn_up) + g_p * (pool @ w_pool_up)
    out = rms_norm(merged @ w_out, g_post)
    return out, k_state, v_state, u_state


def swiglu(x, wg, wu, wd):
    return (jax.nn.silu(x @ wg) * (x @ wu)) @ wd


def moe_swiglu(x, w_router, wg, wu, wd):
    logits = (x @ w_router).astype(jnp.float32)
    top_v, top_i = lax.top_k(logits, TOP_K)
    top_w = jax.nn.softmax(top_v, axis=-1)
    gate = jnp.sum(jax.nn.one_hot(top_i, N_EXPERTS, dtype=jnp.float32) * top_w[..., None], axis=-2)
    y = jnp.zeros(x.shape, jnp.float32)
    for e in range(N_EXPERTS):
        y = y + gate[..., e:e + 1] * swiglu(x, wg[e], wu[e], wd[e]).astype(jnp.float32)
    return y.astype(x.dtype)


def trunk(x, p, pos, caches, prm):
    h = x
    ks, vs, us = [], [], []
    for i in range(DEPTH):
        hist = None if caches is None else (caches[0][i], caches[1][i], caches[2][i])
        mix, k_s, v_s, u_s = token_mixer(
            h, pos, hist, prm['w_in'][i], prm['b_in'][i], prm['attn_sinks'][i],
            prm['w_attn_up'][i], prm['w_pool_group'][i], prm['pool_scale'][i],
            prm['w_pool_up'][i], prm['w_out'][i], prm['g_mix_pre'][i], prm['g_mix_post'][i])
        ks.append(k_s)
        vs.append(v_s)
        us.append(u_s)
        h = h + mix
        hn = rms_norm(h, prm['g_ffn_pre'][i])
        if i % 2 == 0:
            j = i // 2
            f = swiglu(hn, prm['w_gate_dense'][j], prm['w_up_dense'][j], prm['w_down_dense'][j])
        else:
            j = i // 2
            f = moe_swiglu(hn, prm['w_router'][j], prm['w_gate_moe'][j],
                           prm['w_up_moe'][j], prm['w_down_moe'][j])
        h = h + rms_norm(f, prm['g_ffn_post'][i])
        h = h + (p[i] @ prm['w_ple'][i]) * jax.nn.sigmoid(h @ prm['w_ple_gate'][i])
    return h, jnp.stack(ks), jnp.stack(vs), jnp.stack(us)


def setup_inputs(seed: int = 0) -> dict:
    key = jax.random.key(seed)
    ks = jax.random.split(key, 32)
    f32 = jnp.float32

    def nrm(k, shape, scale=1.0):
        return jax.random.normal(k, shape, f32) * scale

    def gain(k, shape):
        return 1.0 + 0.05 * jax.random.normal(k, shape, f32)

    return {
        'x_prompt': nrm(ks[0], (BATCH, SEQ, D_MODEL)),
        'x_sample': nrm(ks[1], (DEC_BATCH, DEC_SEQ, D_MODEL)),
        'cache_k': nrm(ks[2], (DEPTH, DEC_BATCH, WINDOW, N_KV_HEADS, HEAD_DIM)),
        'cache_v': nrm(ks[3], (DEPTH, DEC_BATCH, WINDOW, N_KV_HEADS, HEAD_DIM)),
        'state_pool': nrm(ks[4], (DEPTH, DEC_BATCH, POOL_HIST, POOL_WIDTH)),
        'p_prompt': nrm(ks[5], (DEPTH, BATCH, SEQ, PLE_DIM)),
        'p_sample': nrm(ks[6], (DEPTH, DEC_BATCH, DEC_SEQ, PLE_DIM)),
        'w_in': nrm(ks[7], (DEPTH, D_MODEL, IN_WIDTH), D_MODEL ** -0.5),
        'b_in': nrm(ks[8], (DEPTH, IN_WIDTH), 0.02),
        'attn_sinks': nrm(ks[9], (DEPTH, N_HEADS), 1.0),
        'w_attn_up': nrm(ks[10], (DEPTH, ATTN_WIDTH, D_MODEL), ATTN_WIDTH ** -0.5),
        'w_pool_group': nrm(ks[11], (DEPTH, POOL_GROUPS, POOL_GROUP_DIM, POOL_GROUP_DIM), POOL_GROUP_DIM ** -0.5),
        'pool_scale': gain(ks[12], (DEPTH, POOL_WIDTH)),
        'w_pool_up': nrm(ks[13], (DEPTH, POOL_WIDTH, D_MODEL), POOL_WIDTH ** -0.5),
        'w_out': nrm(ks[14], (DEPTH, D_MODEL, D_MODEL), D_MODEL ** -0.5),
        'g_mix_pre': gain(ks[15], (DEPTH, D_MODEL)),
        'g_mix_post': gain(ks[16], (DEPTH, D_MODEL)),
        'g_ffn_pre': gain(ks[17], (DEPTH, D_MODEL)),
        'g_ffn_post': gain(ks[18], (DEPTH, D_MODEL)),
        'w_gate_dense': nrm(ks[19], (N_DENSE, D_MODEL, D_FF), D_MODEL ** -0.5),
        'w_up_dense': nrm(ks[20], (N_DENSE, D_MODEL, D_FF), D_MODEL ** -0.5),
        'w_down_dense': nrm(ks[21], (N_DENSE, D_FF, D_MODEL), D_FF ** -0.5),
        'w_router': nrm(ks[22], (N_MOE, D_MODEL, N_EXPERTS), D_MODEL ** -0.5),
        'w_gate_moe': nrm(ks[23], (N_MOE, N_EXPERTS, D_MODEL, EXPERT_FF), D_MODEL ** -0.5),
        'w_up_moe': nrm(ks[24], (N_MOE, N_EXPERTS, D_MODEL, EXPERT_FF), D_MODEL ** -0.5),
        'w_down_moe': nrm(ks[25], (N_MOE, N_EXPERTS, EXPERT_FF, D_MODEL), EXPERT_FF ** -0.5),
        'w_ple': nrm(ks[26], (DEPTH, PLE_DIM, D_MODEL), PLE_DIM ** -0.5),
        'w_ple_gate': nrm(ks[27], (DEPTH, D_MODEL, D_MODEL), D_MODEL ** -0.5),
    }


def reference(x_prompt, x_sample, cache_k, cache_v, state_pool, p_prompt, p_sample,
              w_in, b_in, attn_sinks, w_attn_up, w_pool_group, pool_scale, w_pool_up, w_out,
              g_mix_pre, g_mix_post, g_ffn_pre, g_ffn_post,
              w_gate_dense, w_up_dense, w_down_dense,
              w_router, w_gate_moe, w_up_moe, w_down_moe,
              w_ple, w_ple_gate):
    prm = {
        'w_in': w_in, 'b_in': b_in, 'attn_sinks': attn_sinks, 'w_attn_up': w_attn_up,
        'w_pool_group': w_pool_group, 'pool_scale': pool_scale, 'w_pool_up': w_pool_up,
        'w_out': w_out, 'g_mix_pre': g_mix_pre, 'g_mix_post': g_mix_post,
        'g_ffn_pre': g_ffn_pre, 'g_ffn_post': g_ffn_post,
        'w_gate_dense': w_gate_dense, 'w_up_dense': w_up_dense, 'w_down_dense': w_down_dense,
        'w_router': w_router, 'w_gate_moe': w_gate_moe, 'w_up_moe': w_up_moe,
        'w_down_moe': w_down_moe, 'w_ple': w_ple, 'w_ple_gate': w_ple_gate,
    }
    pos_prompt = jnp.arange(x_prompt.shape[1])
    pos_sample = PAST_LEN + jnp.arange(x_sample.shape[1])
    y_prompt, k_prompt, v_prompt, pool_prompt = trunk(x_prompt, p_prompt, pos_prompt, None, prm)
    y_sample, k_sample, v_sample, pool_sample = trunk(
        x_sample, p_sample, pos_sample, (cache_k, cache_v, state_pool), prm)
    return (y_prompt, y_sample, k_prompt, v_prompt, pool_prompt, k_sample, v_sample, pool_sample)
```

```python
import functools
from typing import NamedTuple

import jax
import jax.numpy as jnp
from jax import lax
from jax.experimental import pallas as pl
from jax.experimental.pallas import tpu as pltpu

F32 = jnp.float32
BF16 = jnp.bfloat16

D_MODEL = 1024
CHUNK = 64
WINDOW = 128
N_HEADS = 16
N_KV_HEADS = 2
GROUP = N_HEADS // N_KV_HEADS
HEAD_DIM = 64
ATTN_WIDTH = N_HEADS * HEAD_DIM
KV_WIDTH = N_KV_HEADS * HEAD_DIM
POOL_WINDOWS = (2, 4, 8, 16)
POOL_GROUP_DIM = 128
POOL_WIDTH = len(POOL_WINDOWS) * POOL_GROUP_DIM
POOL_HIST = max(POOL_WINDOWS) - 1
GATE_WIDTH = 2 * D_MODEL
IN_WIDTH = ATTN_WIDTH + 2 * KV_WIDTH + POOL_WIDTH + GATE_WIDTH
N_EXPERTS = 8
PLE_DIM = 256
PAST_LEN = 2048
EPS = 1e-6
NEG_INF = -1e30

LANES = 128
V7X_VMEM_BYTES = 64 * 1024 * 1024
VMEM_LIMIT = V7X_VMEM_BYTES * 7 // 8

HEAD_PAIR = 2 * HEAD_DIM
PAIRS_PER_KV = GROUP // 2
KEY_WIN = 2 * WINDOW
HIST_ROWS = POOL_HIST + 1
COL_CHUNK = 512
FF_CHUNK = 512


def _rms(x, g):
    return x * lax.rsqrt(jnp.mean(x * x, axis=-1, keepdims=True) + EPS) * g


def _dot(a, b):
    return jnp.dot(a, b, preferred_element_type=F32)


def _params(n_grid):
    return pltpu.CompilerParams(dimension_semantics=("arbitrary",) * n_grid,
                                vmem_limit_bytes=VMEM_LIMIT)


def _const_spec(shape):
    return pl.BlockSpec(shape, lambda *_: (0,) * len(shape))


def _in_proj_kernel(h_ref, g_ref, w_ref, b_ref, q_ref, k_ref, v_ref, u_ref, gate_ref):
    xn = _rms(h_ref[...], g_ref[...]).astype(BF16)

    def proj(lo, hi):
        return _dot(xn, w_ref[:, lo:hi]) + b_ref[:, lo:hi]

    c1 = ATTN_WIDTH
    c2 = c1 + KV_WIDTH
    c3 = c2 + KV_WIDTH
    c4 = c3 + POOL_WIDTH
    for lo in range(0, c1, COL_CHUNK):
        q_ref[:, lo:lo + COL_CHUNK] = (proj(lo, lo + COL_CHUNK) * (HEAD_DIM ** -0.5)).astype(BF16)
    kv = proj(c1, c3)
    k_ref[...] = kv[:, :KV_WIDTH]
    v_ref[...] = kv[:, KV_WIDTH:]
    u_ref[...] = proj(c3, c4)
    for lo in range(0, GATE_WIDTH, COL_CHUNK):
        gate_ref[:, lo:lo + COL_CHUNK] = jax.nn.sigmoid(proj(c4 + lo, c4 + lo + COL_CHUNK))


def _in_proj(h, g, w, b, tm):
    n = h.shape[0]
    row = lambda width: pl.BlockSpec((tm, width), lambda i: (i, 0))
    return pl.pallas_call(
        _in_proj_kernel,
        grid=(n // tm,),
        in_specs=[row(D_MODEL), _const_spec((1, D_MODEL)), _const_spec((D_MODEL, IN_WIDTH)),
                  _const_spec((1, IN_WIDTH))],
        out_specs=[row(ATTN_WIDTH), row(KV_WIDTH), row(KV_WIDTH), row(POOL_WIDTH), row(GATE_WIDTH)],
        out_shape=[jax.ShapeDtypeStruct((n, ATTN_WIDTH), BF16),
                   jax.ShapeDtypeStruct((n, KV_WIDTH), F32),
                   jax.ShapeDtypeStruct((n, KV_WIDTH), F32),
                   jax.ShapeDtypeStruct((n, POOL_WIDTH), F32),
                   jax.ShapeDtypeStruct((n, GATE_WIDTH), F32)],
        compiler_params=_params(1),
    )(h, g, w, b)


class MixerCfg(NamedTuple):
    lq: int
    n_units: int
    win_stride: int
    n_seg: int
    seg_len: int
    has_halo: bool
    pos0: int

    @property
    def tb(self):
        return self.lq * self.n_units

    @property
    def key_rows(self):
        return (self.n_units - 1) * self.win_stride + KEY_WIN


def _fill_block_diag(dst_ref, x):
    lane = lax.broadcasted_iota(jnp.int32, x.shape, 1)
    lo = lane < HEAD_DIM
    xr = pltpu.roll(x, HEAD_DIM, 1)
    zero = jnp.zeros_like(x)
    dst_ref[0, 0] = jnp.where(lo, x, zero).astype(BF16)
    dst_ref[0, 1] = jnp.where(lo, zero, xr).astype(BF16)
    dst_ref[1, 0] = jnp.where(lo, xr, zero).astype(BF16)
    dst_ref[1, 1] = jnp.where(lo, zero, x).astype(BF16)


def _attention(cfg, q_ref, kbd_ref, vbd_ref, bias_ref, sink_ref, attn_ref, first_block):
    lq = cfg.lq

    def unit(n, carry):
        off_q = pl.multiple_of(n * lq, lq)
        off_w = pl.multiple_of(n * cfg.win_stride, LANES)
        bias_idx = jnp.where(jnp.logical_and(first_block, n == 0), 1, 0) if cfg.has_halo else 0
        for kv in range(N_KV_HEADS):
            cols = [LANES * (PAIRS_PER_KV * kv + p) for p in range(PAIRS_PER_KV)]
            q2 = jnp.concatenate([q_ref[pl.ds(off_q, lq), c:c + HEAD_PAIR] for c in cols], axis=0)
            kbd = jnp.concatenate([kbd_ref[kv, 0, pl.ds(off_w, KEY_WIN), :],
                                   kbd_ref[kv, 1, pl.ds(off_w, KEY_WIN), :]], axis=0)
            vbd = jnp.concatenate([vbd_ref[kv, 0, pl.ds(off_w, KEY_WIN), :],
                                   vbd_ref[kv, 1, pl.ds(off_w, KEY_WIN), :]], axis=0)
            s_all = lax.dot_general(q2, kbd, (((1,), (1,)), ((), ())), preferred_element_type=F32)
            probs, dens = [], []
            for p in range(PAIRS_PER_KV):
                pair_e = []
                for j in range(2):
                    head = kv * GROUP + 2 * p + j
                    s = s_all[p * lq:(p + 1) * lq, j * KEY_WIN:(j + 1) * KEY_WIN] + bias_ref[bias_idx, head]
                    sink = sink_ref[head]
                    m = jnp.maximum(jnp.max(s, axis=-1, keepdims=True), sink)
                    e = jnp.exp(s - m)
                    dens.append(jnp.sum(e, axis=-1, keepdims=True) + jnp.exp(sink - m))
                    pair_e.append(e.astype(BF16))
                probs.append(jnp.concatenate(pair_e, axis=1))
            o_all = _dot(jnp.concatenate(probs, axis=0), vbd)
            lane = lax.broadcasted_iota(jnp.int32, (lq, HEAD_PAIR), 1)
            for p in range(PAIRS_PER_KV):
                den = jnp.where(lane < HEAD_DIM, dens[2 * p], dens[2 * p + 1])
                o = o_all[p * lq:(p + 1) * lq] / den
                attn_ref[pl.ds(off_q, lq), cols[p]:cols[p] + HEAD_PAIR] = o.astype(BF16)
        return carry

    lax.fori_loop(0, cfg.n_units, unit, 0)


def _pooling(cfg, ext_ref, wpg_ref, scale_ref, pool_ref, pos_base):
    ln = cfg.seg_len
    row = lax.broadcasted_iota(jnp.int32, (ln, 1), 0)
    for seg in range(cfg.n_seg):
        pos = pos_base + row
        for g, w in enumerate(POOL_WINDOWS):
            lanes = slice(g * POOL_GROUP_DIM, (g + 1) * POOL_GROUP_DIM)
            u = ext_ref[seg, HIST_ROWS:HIST_ROWS + ln, lanes]
            acc = u
            for back in range(1, w):
                acc = acc + ext_ref[seg, HIST_ROWS - back:HIST_ROWS - back + ln, lanes]
            cnt = jnp.minimum(pos + 1, w).astype(F32)
            mixed = (acc / cnt - u).astype(BF16)
            y = _dot(mixed, wpg_ref[g]) * scale_ref[:, lanes]
            pool_ref[seg * ln:(seg + 1) * ln, lanes] = y.astype(BF16)


def _mixer_kernel(cfg, *refs):
    if cfg.has_halo:
        (sink_ref, q_ref, kh_ref, k_ref, vh_ref, v_ref, uh_ref, u_ref, gate_ref, h_ref, bias_ref,
         wau_ref, wpg_ref, scale_ref, wpu_ref, wout_ref, gpost_ref, out_ref,
         kbd_ref, vbd_ref, ext_ref, attn_ref, pool_ref) = refs
    else:
        (sink_ref, q_ref, k_ref, v_ref, uh_ref, u_ref, gate_ref, h_ref, bias_ref,
         wau_ref, wpg_ref, scale_ref, wpu_ref, wout_ref, gpost_ref, out_ref,
         kbd_ref, vbd_ref, ext_ref, attn_ref, pool_ref) = refs

    step = pl.program_id(1) if cfg.has_halo else pl.program_id(0)
    first_block = step == 0

    if cfg.has_halo:
        _fill_block_diag(kbd_ref, jnp.concatenate([kh_ref[...], k_ref[...]], axis=0))
        _fill_block_diag(vbd_ref, jnp.concatenate([vh_ref[...], v_ref[...]], axis=0))
    else:
        _fill_block_diag(kbd_ref, k_ref[...])
        _fill_block_diag(vbd_ref, v_ref[...])
    _attention(cfg, q_ref, kbd_ref, vbd_ref, bias_ref, sink_ref, attn_ref, first_block)

    if cfg.has_halo:
        hist = uh_ref[...]
        ext_ref[0, 0:HIST_ROWS, :] = jnp.where(first_block, jnp.zeros_like(hist), hist)
        ext_ref[0, HIST_ROWS:, :] = u_ref[...]
        pos_base = cfg.pos0 + step * cfg.tb
    else:
        for seg in range(cfg.n_seg):
            ext_ref[seg, 0:HIST_ROWS, :] = uh_ref[seg]
            ext_ref[seg, HIST_ROWS:, :] = u_ref[seg * cfg.seg_len:(seg + 1) * cfg.seg_len, :]
        pos_base = cfg.pos0
    _pooling(cfg, ext_ref, wpg_ref, scale_ref, pool_ref, pos_base)

    a = _dot(attn_ref[...], wau_ref[...])
    p = _dot(pool_ref[...], wpu_ref[...])
    merged = gate_ref[:, :D_MODEL] * a + gate_ref[:, D_MODEL:] * p
    o = _dot(merged.astype(BF16), wout_ref[...])
    out_ref[...] = h_ref[...] + _rms(o, gpost_ref[...])


def _mixer_weights_specs():
    return [_const_spec((ATTN_WIDTH, D_MODEL)),
            _const_spec((len(POOL_WINDOWS), POOL_GROUP_DIM, POOL_GROUP_DIM)),
            _const_spec((1, POOL_WIDTH)),
            _const_spec((POOL_WIDTH, D_MODEL)),
            _const_spec((D_MODEL, D_MODEL)),
            _const_spec((1, D_MODEL))]


def _mixer_scratch(cfg):
    return [pltpu.VMEM((N_KV_HEADS, 2, cfg.key_rows, KV_WIDTH), BF16),
            pltpu.VMEM((N_KV_HEADS, 2, cfg.key_rows, KV_WIDTH), BF16),
            pltpu.VMEM((cfg.n_seg, HIST_ROWS + cfg.seg_len, POOL_WIDTH), F32),
            pltpu.VMEM((cfg.tb, ATTN_WIDTH), BF16),
            pltpu.VMEM((cfg.tb, POOL_WIDTH), BF16)]


def _mixer_prompt(cfg, sinks, q, k, v, u, gates, h, bias, weights):
    b, t = q.shape[:2]
    tb = cfg.tb
    halo_k = tb // WINDOW
    halo_u = tb // HIST_ROWS
    cur = lambda width: pl.BlockSpec((None, tb, width), lambda bi, i: (bi, i, 0))
    prev_k = pl.BlockSpec((None, WINDOW, KV_WIDTH), lambda bi, i: (bi, jnp.maximum(i * halo_k - 1, 0), 0))
    prev_u = pl.BlockSpec((None, HIST_ROWS, POOL_WIDTH), lambda bi, i: (bi, jnp.maximum(i * halo_u - 1, 0), 0))
    return pl.pallas_call(
        functools.partial(_mixer_kernel, cfg),
        grid=(b, t // tb),
        in_specs=[pl.BlockSpec(memory_space=pltpu.SMEM),
                  cur(ATTN_WIDTH), prev_k, cur(KV_WIDTH), prev_k, cur(KV_WIDTH),
                  prev_u, cur(POOL_WIDTH), cur(GATE_WIDTH), cur(D_MODEL),
                  _const_spec(bias.shape)] + _mixer_weights_specs(),
        out_specs=cur(D_MODEL),
        out_shape=jax.ShapeDtypeStruct((b, t, D_MODEL), F32),
        scratch_shapes=_mixer_scratch(cfg),
        compiler_params=_params(2),
    )(sinks, q, k, k, v, v, u, u, gates, h, bias, *weights)


def _mixer_sample(cfg, sinks, q, k_win, v_win, u_hist, u, gates, h, bias, weights):
    n = q.shape[0]
    tb = cfg.tb
    row = lambda width: pl.BlockSpec((tb, width), lambda i: (i, 0))
    return pl.pallas_call(
        functools.partial(_mixer_kernel, cfg),
        grid=(n // tb,),
        in_specs=[pl.BlockSpec(memory_space=pltpu.SMEM),
                  row(ATTN_WIDTH),
                  pl.BlockSpec((cfg.key_rows, KV_WIDTH), lambda i: (i, 0)),
                  pl.BlockSpec((cfg.key_rows, KV_WIDTH), lambda i: (i, 0)),
                  pl.BlockSpec((cfg.n_seg, HIST_ROWS, POOL_WIDTH), lambda i: (i, 0, 0)),
                  row(POOL_WIDTH), row(GATE_WIDTH), row(D_MODEL),
                  _const_spec(bias.shape)] + _mixer_weights_specs(),
        out_specs=row(D_MODEL),
        out_shape=jax.ShapeDtypeStruct((n, D_MODEL), F32),
        scratch_shapes=_mixer_scratch(cfg),
        compiler_params=_params(1),
    )(sinks, q, k_win, v_win, u_hist, u, gates, h, bias, *weights)


def _alibi_bias(lq, q_off, valid_fn):
    slopes = 2.0 ** (-8.0 * jnp.arange(1, N_HEADS + 1, dtype=F32) / N_HEADS)
    r = jnp.arange(lq)[:, None]
    j = jnp.arange(KEY_WIN)[None, :]
    dist = jnp.abs(r + q_off - j).astype(F32)
    tables = []
    for valid in valid_fn(r, j):
        tables.append(jnp.where(valid[None], -(slopes[:, None, None] * dist[None]), NEG_INF))
    return jnp.stack(tables)


def _ffn_kernel(moe, *refs):
    if moe:
        (h_ref, gpre_ref, wr_ref, wg_ref, wu_ref, wd_ref, gpost_ref, p_ref, wple_ref, wpg_ref,
         out_ref, hn_ref, acc_ref, route_ref) = refs
    else:
        (h_ref, gpre_ref, wg_ref, wu_ref, wd_ref, gpost_ref, p_ref, wple_ref, wpg_ref,
         out_ref, hn_ref, acc_ref) = refs
    e = pl.program_id(1)
    f = pl.program_id(2)
    first = jnp.logical_and(e == 0, f == 0)
    last = jnp.logical_and(e == pl.num_programs(1) - 1, f == pl.num_programs(2) - 1)

    @pl.when(first)
    def _():
        hn = _rms(h_ref[...], gpre_ref[...]).astype(BF16)
        hn_ref[...] = hn
        acc_ref[...] = jnp.zeros_like(acc_ref)
        if moe:
            logits = _dot(hn, wr_ref[...])
            lane = lax.broadcasted_iota(jnp.int32, logits.shape, 1).astype(F32)
            logits = jnp.where(lane < N_EXPERTS, logits, -jnp.inf)
            m1 = jnp.max(logits, axis=-1, keepdims=True)
            i1 = jnp.min(jnp.where(logits == m1, lane, float(LANES)), axis=-1, keepdims=True)
            rest = jnp.where(lane == i1, -jnp.inf, logits)
            m2 = jnp.max(rest, axis=-1, keepdims=True)
            i2 = jnp.min(jnp.where(rest == m2, lane, float(LANES)), axis=-1, keepdims=True)
            t = jnp.exp(m2 - m1)
            den = 1.0 + t
            route_ref[...] = jnp.where(lane == i1, 1.0 / den, 0.0) + jnp.where(lane == i2, t / den, 0.0)

    hn = hn_ref[...]
    act = (jax.nn.silu(_dot(hn, wg_ref[...])) * _dot(hn, wu_ref[...])).astype(BF16)
    y = _dot(act, wd_ref[...])
    if moe:
        lane = lax.broadcasted_iota(jnp.int32, route_ref.shape, 1)
        gate = jnp.sum(jnp.where(lane == e, route_ref[...], 0.0), axis=-1, keepdims=True)
        y = gate * y
    acc_ref[...] += y

    @pl.when(last)
    def _():
        h2 = h_ref[...] + _rms(acc_ref[...], gpost_ref[...])
        emb = _dot(p_ref[...].astype(BF16), wple_ref[...])
        out_ref[...] = h2 + emb * jax.nn.sigmoid(_dot(h2.astype(BF16), wpg_ref[...]))


def _ffn(h, gpre, w_router, wg, wu, wd, gpost, p, w_ple, w_ple_gate, tm):
    n = h.shape[0]
    n_exp, _, ff = wg.shape
    moe = w_router is not None
    row = lambda width: pl.BlockSpec((tm, width), lambda i, e, f: (i, 0))
    const = lambda shape: pl.BlockSpec(shape, lambda i, e, f: (0,) * len(shape))
    in_specs = [row(D_MODEL), const((1, D_MODEL))]
    args = [h, gpre]
    scratch = [pltpu.VMEM((tm, D_MODEL), BF16), pltpu.VMEM((tm, D_MODEL), F32)]
    if moe:
        in_specs.append(const((D_MODEL, LANES)))
        args.append(w_router)
        scratch.append(pltpu.VMEM((tm, LANES), F32))
    in_specs += [pl.BlockSpec((None, D_MODEL, FF_CHUNK), lambda i, e, f: (e, 0, f)),
                 pl.BlockSpec((None, D_MODEL, FF_CHUNK), lambda i, e, f: (e, 0, f)),
                 pl.BlockSpec((None, FF_CHUNK, D_MODEL), lambda i, e, f: (e, f, 0)),
                 const((1, D_MODEL)), row(PLE_DIM), const((PLE_DIM, D_MODEL)), const((D_MODEL, D_MODEL))]
    args += [wg, wu, wd, gpost, p, w_ple, w_ple_gate]
    return pl.pallas_call(
        functools.partial(_ffn_kernel, moe),
        grid=(n // tm, n_exp, ff // FF_CHUNK),
        in_specs=in_specs,
        out_specs=row(D_MODEL),
        out_shape=jax.ShapeDtypeStruct((n, D_MODEL), F32),
        scratch_shapes=scratch,
        compiler_params=_params(3),
    )(*args)


def _row_tile(n, target):
    tm = min(n, target)
    assert n % tm == 0, (n, tm)
    return tm


def _trunk(x, p, caches, prm, pos0):
    b, t, _ = x.shape
    n = b * t
    depth = prm['w_in'].shape[0]
    h = x.reshape(n, D_MODEL)
    vec = lambda a: a.reshape(1, -1)
    ks, vs, us = [], [], []

    if caches is None:
        cfg = MixerCfg(lq=2 * CHUNK, n_units=4, win_stride=WINDOW, n_seg=1, seg_len=8 * CHUNK,
                       has_halo=True, pos0=pos0)
        assert t % cfg.tb == 0
        bias = _alibi_bias(cfg.lq, WINDOW, lambda r, j: (
            (j // CHUNK >= r // CHUNK) & (j // CHUNK <= r // CHUNK + 2),
            (j // CHUNK >= r // CHUNK) & (j // CHUNK <= r // CHUNK + 2) & (j >= WINDOW)))
    else:
        seqs = 8
        cfg = MixerCfg(lq=t, n_units=seqs, win_stride=KEY_WIN, n_seg=seqs, seg_len=t,
                       has_halo=False, pos0=pos0)
        assert b % seqs == 0 and WINDOW + t <= KEY_WIN
        bias = _alibi_bias(cfg.lq, WINDOW, lambda r, j: ((j < WINDOW + t) & (r >= 0),))

    for i in range(depth):
        q, k, v, u, gates = _in_proj(h, vec(prm['g_mix_pre'][i]), prm['w_in'][i].astype(BF16),
                                     vec(prm['b_in'][i]), _row_tile(n, 512))
        weights = (prm['w_attn_up'][i].astype(BF16), prm['w_pool_group'][i].astype(BF16),
                   vec(prm['pool_scale'][i]), prm['w_pool_up'][i].astype(BF16),
                   prm['w_out'][i].astype(BF16), vec(prm['g_mix_post'][i]))
        sinks = prm['attn_sinks'][i]
        k3 = k.reshape(b, t, KV_WIDTH)
        v3 = v.reshape(b, t, KV_WIDTH)
        u3 = u.reshape(b, t, POOL_WIDTH)
        if caches is None:
            h = _mixer_prompt(cfg, sinks, q.reshape(b, t, ATTN_WIDTH), k3, v3, u3,
                              gates.reshape(b, t, GATE_WIDTH), h.reshape(b, t, D_MODEL),
                              bias, weights).reshape(n, D_MODEL)
            k_all, v_all, u_all = k3, v3, u3
        else:
            k_all = jnp.concatenate([caches[0][i].reshape(b, WINDOW, KV_WIDTH), k3], axis=1)
            v_all = jnp.concatenate([caches[1][i].reshape(b, WINDOW, KV_WIDTH), v3], axis=1)
            u_all = jnp.concatenate([caches[2][i], u3], axis=1)
            pad = ((0, 0), (0, KEY_WIN - WINDOW - t), (0, 0))
            k_win = jnp.pad(k_all, pad).reshape(b * KEY_WIN, KV_WIDTH)
            v_win = jnp.pad(v_all, pad).reshape(b * KEY_WIN, KV_WIDTH)
            u_hist = jnp.pad(caches[2][i], ((0, 0), (HIST_ROWS - POOL_HIST, 0), (0, 0)))
            h = _mixer_sample(cfg, sinks, q, k_win, v_win, u_hist, u, gates, h, bias, weights)
        ks.append(k_all[:, -WINDOW:].reshape(b, WINDOW, N_KV_HEADS, HEAD_DIM))
        vs.append(v_all[:, -WINDOW:].reshape(b, WINDOW, N_KV_HEADS, HEAD_DIM))
        us.append(u_all[:, -POOL_HIST:])

        j = i // 2
        if i % 2 == 0:
            router = None
            wg, wu, wd = (prm['w_gate_dense'][j][None], prm['w_up_dense'][j][None],
                          prm['w_down_dense'][j][None])
        else:
            router = jnp.pad(prm['w_router'][j], ((0, 0), (0, LANES - N_EXPERTS))).astype(BF16)
            wg, wu, wd = prm['w_gate_moe'][j], prm['w_up_moe'][j], prm['w_down_moe'][j]
        h = _ffn(h, vec(prm['g_ffn_pre'][i]), router, wg.astype(BF16), wu.astype(BF16), wd.astype(BF16),
                 vec(prm['g_ffn_post'][i]), p[i].reshape(n, PLE_DIM), prm['w_ple'][i].astype(BF16),
                 prm['w_ple_gate'][i].astype(BF16), _row_tile(n, 1024 if router is not None else 512))
    return h.reshape(b, t, D_MODEL), jnp.stack(ks), jnp.stack(vs), jnp.stack(us)


def kernel(x_prompt, x_sample, cache_k, cache_v, state_pool, p_prompt, p_sample, w_in, b_in, attn_sinks, w_attn_up, w_pool_group, pool_scale, w_pool_up, w_out, g_mix_pre, g_mix_post, g_ffn_pre, g_ffn_post, w_gate_dense, w_up_dense, w_down_dense, w_router, w_gate_moe, w_up_moe, w_down_moe, w_ple, w_ple_gate):
    prm = {
        'w_in': w_in, 'b_in': b_in, 'attn_sinks': attn_sinks, 'w_attn_up': w_attn_up,
        'w_pool_group': w_pool_group, 'pool_scale': pool_scale, 'w_pool_up': w_pool_up,
        'w_out': w_out, 'g_mix_pre': g_mix_pre, 'g_mix_post': g_mix_post,
        'g_ffn_pre': g_ffn_pre, 'g_ffn_post': g_ffn_post,
        'w_gate_dense': w_gate_dense, 'w_up_dense': w_up_dense, 'w_down_dense': w_down_dense,
        'w_router': w_router, 'w_gate_moe': w_gate_moe, 'w_up_moe': w_up_moe,
        'w_down_moe': w_down_moe, 'w_ple': w_ple, 'w_ple_gate': w_ple_gate,
    }
    y_prompt, k_prompt, v_prompt, pool_prompt = _trunk(x_prompt, p_prompt, None, prm, 0)
    y_sample, k_sample, v_sample, pool_sample = _trunk(
        x_sample, p_sample, (cache_k, cache_v, state_pool), prm, PAST_LEN)
    return (y_prompt, y_sample, k_prompt, v_prompt, pool_prompt, k_sample, v_sample, pool_sample)
```

```python
import functools
from typing import NamedTuple

import jax
import jax.numpy as jnp
from jax import lax
from jax.experimental import pallas as pl
from jax.experimental.pallas import tpu as pltpu

F32 = jnp.float32
BF16 = jnp.bfloat16

D_MODEL = 1024
CHUNK = 64
WINDOW = 128
N_HEADS = 16
N_KV_HEADS = 2
GROUP = N_HEADS // N_KV_HEADS
HEAD_DIM = 64
ATTN_WIDTH = N_HEADS * HEAD_DIM
KV_WIDTH = N_KV_HEADS * HEAD_DIM
POOL_WINDOWS = (2, 4, 8, 16)
POOL_GROUP_DIM = 128
POOL_WIDTH = len(POOL_WINDOWS) * POOL_GROUP_DIM
POOL_HIST = max(POOL_WINDOWS) - 1
GATE_WIDTH = 2 * D_MODEL
IN_WIDTH = ATTN_WIDTH + 2 * KV_WIDTH + POOL_WIDTH + GATE_WIDTH
N_EXPERTS = 8
PLE_DIM = 256
PAST_LEN = 2048
EPS = 1e-6
NEG_INF = -1e30

LANES = 128
V7X_VMEM_BYTES = 64 * 1024 * 1024
VMEM_LIMIT = V7X_VMEM_BYTES * 7 // 8

HEAD_PAIR = 2 * HEAD_DIM
PAIRS_PER_KV = GROUP // 2
KEY_WIN = 2 * WINDOW
HIST_ROWS = POOL_HIST + 1
COL_CHUNK = 512
FF_CHUNK = 512


def _rms(x, g):
    return x * lax.rsqrt(jnp.mean(x * x, axis=-1, keepdims=True) + EPS) * g


def _dot(a, b):
    return jnp.dot(a, b, preferred_element_type=F32)


def _params(n_grid):
    return pltpu.CompilerParams(dimension_semantics=("arbitrary",) * n_grid,
                                vmem_limit_bytes=VMEM_LIMIT)


def _const_spec(shape):
    return pl.BlockSpec(shape, lambda *_: (0,) * len(shape))


def _in_proj_kernel(h_ref, g_ref, w_ref, b_ref, q_ref, k_ref, v_ref, u_ref, gate_ref):
    xn = _rms(h_ref[...], g_ref[...]).astype(BF16)

    def proj(lo, hi):
        return _dot(xn, w_ref[:, lo:hi]) + b_ref[:, lo:hi]

    c1 = ATTN_WIDTH
    c2 = c1 + KV_WIDTH
    c3 = c2 + KV_WIDTH
    c4 = c3 + POOL_WIDTH
    for lo in range(0, c1, COL_CHUNK):
        q_ref[:, lo:lo + COL_CHUNK] = (proj(lo, lo + COL_CHUNK) * (HEAD_DIM ** -0.5)).astype(BF16)
    kv = proj(c1, c3)
    k_ref[...] = kv[:, :KV_WIDTH]
    v_ref[...] = kv[:, KV_WIDTH:]
    u_ref[...] = proj(c3, c4)
    for lo in range(0, GATE_WIDTH, COL_CHUNK):
        gate_ref[:, lo:lo + COL_CHUNK] = jax.nn.sigmoid(proj(c4 + lo, c4 + lo + COL_CHUNK))


def _in_proj(h, g, w, b, tm):
    n = h.shape[0]
    row = lambda width: pl.BlockSpec((tm, width), lambda i: (i, 0))
    return pl.pallas_call(
        _in_proj_kernel,
        grid=(n // tm,),
        in_specs=[row(D_MODEL), _const_spec((1, D_MODEL)), _const_spec((D_MODEL, IN_WIDTH)),
                  _const_spec((1, IN_WIDTH))],
        out_specs=[row(ATTN_WIDTH), row(KV_WIDTH), row(KV_WIDTH), row(POOL_WIDTH), row(GATE_WIDTH)],
        out_shape=[jax.ShapeDtypeStruct((n, ATTN_WIDTH), BF16),
                   jax.ShapeDtypeStruct((n, KV_WIDTH), F32),
                   jax.ShapeDtypeStruct((n, KV_WIDTH), F32),
                   jax.ShapeDtypeStruct((n, POOL_WIDTH), F32),
                   jax.ShapeDtypeStruct((n, GATE_WIDTH), F32)],
        compiler_params=_params(1),
    )(h, g, w, b)


class MixerCfg(NamedTuple):
    lq: int
    n_units: int
    win_stride: int
    n_seg: int
    seg_len: int
    has_halo: bool
    pos0: int

    @property
    def tb(self):
        return self.lq * self.n_units

    @property
    def key_rows(self):
        return (self.n_units - 1) * self.win_stride + KEY_WIN


def _fill_block_diag(dst_ref, x):
    lane = lax.broadcasted_iota(jnp.int32, x.shape, 1)
    lo = lane < HEAD_DIM
    xr = pltpu.roll(x, HEAD_DIM, 1)
    zero = jnp.zeros_like(x)
    dst_ref[0, 0] = jnp.where(lo, x, zero).astype(BF16)
    dst_ref[0, 1] = jnp.where(lo, zero, xr).astype(BF16)
    dst_ref[1, 0] = jnp.where(lo, xr, zero).astype(BF16)
    dst_ref[1, 1] = jnp.where(lo, zero, x).astype(BF16)


def _attention(cfg, q_ref, kbd_ref, vbd_ref, bias_ref, sink_ref, attn_ref, first_block):
    lq = cfg.lq

    def unit(n, carry):
        off_q = pl.multiple_of(n * lq, lq)
        off_w = pl.multiple_of(n * cfg.win_stride, LANES)
        bias_idx = jnp.where(jnp.logical_and(first_block, n == 0), 1, 0) if cfg.has_halo else 0
        for kv in range(N_KV_HEADS):
            cols = [LANES * (PAIRS_PER_KV * kv + p) for p in range(PAIRS_PER_KV)]
            q2 = jnp.concatenate([q_ref[pl.ds(off_q, lq), c:c + HEAD_PAIR] for c in cols], axis=0)
            kbd = jnp.concatenate([kbd_ref[kv, 0, pl.ds(off_w, KEY_WIN), :],
                                   kbd_ref[kv, 1, pl.ds(off_w, KEY_WIN), :]], axis=0)
            vbd = jnp.concatenate([vbd_ref[kv, 0, pl.ds(off_w, KEY_WIN), :],
                                   vbd_ref[kv, 1, pl.ds(off_w, KEY_WIN), :]], axis=0)
            s_all = lax.dot_general(q2, kbd, (((1,), (1,)), ((), ())), preferred_element_type=F32)
            probs, dens = [], []
            for p in range(PAIRS_PER_KV):
                pair_e = []
                for j in range(2):
                    head = kv * GROUP + 2 * p + j
                    s = s_all[p * lq:(p + 1) * lq, j * KEY_WIN:(j + 1) * KEY_WIN] + bias_ref[bias_idx, head]
                    sink = sink_ref[head]
                    m = jnp.maximum(jnp.max(s, axis=-1, keepdims=True), sink)
                    e = jnp.exp(s - m)
                    dens.append(jnp.sum(e, axis=-1, keepdims=True) + jnp.exp(sink - m))
                    pair_e.append(e.astype(BF16))
                probs.append(jnp.concatenate(pair_e, axis=1))
            o_all = _dot(jnp.concatenate(probs, axis=0), vbd)
            lane = lax.broadcasted_iota(jnp.int32, (lq, HEAD_PAIR), 1)
            for p in range(PAIRS_PER_KV):
                den = jnp.where(lane < HEAD_DIM, dens[2 * p], dens[2 * p + 1])
                o = o_all[p * lq:(p + 1) * lq] / den
                attn_ref[pl.ds(off_q, lq), cols[p]:cols[p] + HEAD_PAIR] = o.astype(BF16)
        return carry

    lax.fori_loop(0, cfg.n_units, unit, 0)


def _pooling(cfg, ext_ref, wpg_ref, scale_ref, pool_ref, pos_base):
    ln = cfg.seg_len
    row = lax.broadcasted_iota(jnp.int32, (ln, 1), 0)
    for seg in range(cfg.n_seg):
        pos = pos_base + row
        for g, w in enumerate(POOL_WINDOWS):
            lanes = slice(g * POOL_GROUP_DIM, (g + 1) * POOL_GROUP_DIM)
            u = ext_ref[seg, HIST_ROWS:HIST_ROWS + ln, lanes]
            acc = u
            for back in range(1, w):
                acc = acc + ext_ref[seg, HIST_ROWS - back:HIST_ROWS - back + ln, lanes]
            cnt = jnp.minimum(pos + 1, w).astype(F32)
            mixed = (acc / cnt - u).astype(BF16)
            y = _dot(mixed, wpg_ref[g]) * scale_ref[:, lanes]
            pool_ref[seg * ln:(seg + 1) * ln, lanes] = y.astype(BF16)


def _mixer_kernel(cfg, *refs):
    if cfg.has_halo:
        (sink_ref, q_ref, kh_ref, k_ref, vh_ref, v_ref, uh_ref, u_ref, gate_ref, h_ref, bias_ref,
         wau_ref, wpg_ref, scale_ref, wpu_ref, wout_ref, gpost_ref, out_ref,
         kbd_ref, vbd_ref, ext_ref, attn_ref, pool_ref) = refs
    else:
        (sink_ref, q_ref, k_ref, v_ref, uh_ref, u_ref, gate_ref, h_ref, bias_ref,
         wau_ref, wpg_ref, scale_ref, wpu_ref, wout_ref, gpost_ref, out_ref,
         kbd_ref, vbd_ref, ext_ref, attn_ref, pool_ref) = refs

    step = pl.program_id(1) if cfg.has_halo else pl.program_id(0)
    first_block = step == 0

    if cfg.has_halo:
        _fill_block_diag(kbd_ref, jnp.concatenate([kh_ref[...], k_ref[...]], axis=0))
        _fill_block_diag(vbd_ref, jnp.concatenate([vh_ref[...], v_ref[...]], axis=0))
    else:
        _fill_block_diag(kbd_ref, k_ref[...])
        _fill_block_diag(vbd_ref, v_ref[...])
    _attention(cfg, q_ref, kbd_ref, vbd_ref, bias_ref, sink_ref, attn_ref, first_block)

    if cfg.has_halo:
        hist = uh_ref[...]
        ext_ref[0, 0:HIST_ROWS, :] = jnp.where(first_block, jnp.zeros_like(hist), hist)
        ext_ref[0, HIST_ROWS:, :] = u_ref[...]
        pos_base = cfg.pos0 + step * cfg.tb
    else:
        for seg in range(cfg.n_seg):
            ext_ref[seg, 0:HIST_ROWS, :] = uh_ref[seg]
            ext_ref[seg, HIST_ROWS:, :] = u_ref[seg * cfg.seg_len:(seg + 1) * cfg.seg_len, :]
        pos_base = cfg.pos0
    _pooling(cfg, ext_ref, wpg_ref, scale_ref, pool_ref, pos_base)

    a = _dot(attn_ref[...], wau_ref[...])
    p = _dot(pool_ref[...], wpu_ref[...])
    merged = gate_ref[:, :D_MODEL] * a + gate_ref[:, D_MODEL:] * p
    o = _dot(merged.astype(BF16), wout_ref[...])
    out_ref[...] = h_ref[...] + _rms(o, gpost_ref[...])


def _mixer_weights_specs():
    return [_const_spec((ATTN_WIDTH, D_MODEL)),
            _const_spec((len(POOL_WINDOWS), POOL_GROUP_DIM, POOL_GROUP_DIM)),
            _const_spec((1, POOL_WIDTH)),
            _const_spec((POOL_WIDTH, D_MODEL)),
            _const_spec((D_MODEL, D_MODEL)),
            _const_spec((1, D_MODEL))]


def _mixer_scratch(cfg):
    return [pltpu.VMEM((N_KV_HEADS, 2, cfg.key_rows, KV_WIDTH), BF16),
            pltpu.VMEM((N_KV_HEADS, 2, cfg.key_rows, KV_WIDTH), BF16),
            pltpu.VMEM((cfg.n_seg, HIST_ROWS + cfg.seg_len, POOL_WIDTH), F32),
            pltpu.VMEM((cfg.tb, ATTN_WIDTH), BF16),
            pltpu.VMEM((cfg.tb, POOL_WIDTH), BF16)]


def _mixer_prompt(cfg, sinks, q, k, v, u, gates, h, bias, weights):
    b, t = q.shape[:2]
    tb = cfg.tb
    halo_k = tb // WINDOW
    halo_u = tb // HIST_ROWS
    cur = lambda width: pl.BlockSpec((None, tb, width), lambda bi, i: (bi, i, 0))
    prev_k = pl.BlockSpec((None, WINDOW, KV_WIDTH), lambda bi, i: (bi, jnp.maximum(i * halo_k - 1, 0), 0))
    prev_u = pl.BlockSpec((None, HIST_ROWS, POOL_WIDTH), lambda bi, i: (bi, jnp.maximum(i * halo_u - 1, 0), 0))
    return pl.pallas_call(
        functools.partial(_mixer_kernel, cfg),
        grid=(b, t // tb),
        in_specs=[pl.BlockSpec(memory_space=pltpu.SMEM),
                  cur(ATTN_WIDTH), prev_k, cur(KV_WIDTH), prev_k, cur(KV_WIDTH),
                  prev_u, cur(POOL_WIDTH), cur(GATE_WIDTH), cur(D_MODEL),
                  _const_spec(bias.shape)] + _mixer_weights_specs(),
        out_specs=cur(D_MODEL),
        out_shape=jax.ShapeDtypeStruct((b, t, D_MODEL), F32),
        scratch_shapes=_mixer_scratch(cfg),
        compiler_params=_params(2),
    )(sinks, q, k, k, v, v, u, u, gates, h, bias, *weights)


def _mixer_sample(cfg, sinks, q, k_win, v_win, u_hist, u, gates, h, bias, weights):
    n = q.shape[0]
    tb = cfg.tb
    row = lambda width: pl.BlockSpec((tb, width), lambda i: (i, 0))
    return pl.pallas_call(
        functools.partial(_mixer_kernel, cfg),
        grid=(n // tb,),
        in_specs=[pl.BlockSpec(memory_space=pltpu.SMEM),
                  row(ATTN_WIDTH),
                  pl.BlockSpec((cfg.key_rows, KV_WIDTH), lambda i: (i, 0)),
                  pl.BlockSpec((cfg.key_rows, KV_WIDTH), lambda i: (i, 0)),
                  pl.BlockSpec((cfg.n_seg, HIST_ROWS, POOL_WIDTH), lambda i: (i, 0, 0)),
                  row(POOL_WIDTH), row(GATE_WIDTH), row(D_MODEL),
                  _const_spec(bias.shape)] + _mixer_weights_specs(),
        out_specs=row(D_MODEL),
        out_shape=jax.ShapeDtypeStruct((n, D_MODEL), F32),
        scratch_shapes=_mixer_scratch(cfg),
        compiler_params=_params(1),
    )(sinks, q, k_win, v_win, u_hist, u, gates, h, bias, *weights)


def _alibi_bias(lq, q_off, valid_fn):
    slopes = 2.0 ** (-8.0 * jnp.arange(1, N_HEADS + 1, dtype=F32) / N_HEADS)
    r = jnp.arange(lq)[:, None]
    j = jnp.arange(KEY_WIN)[None, :]
    dist = jnp.abs(r + q_off - j).astype(F32)
    tables = []
    for valid in valid_fn(r, j):
        tables.append(jnp.where(valid[None], -(slopes[:, None, None] * dist[None]), NEG_INF))
    return jnp.stack(tables)


def _swiglu_chunk(x, wg_ref, wu_ref, wd_ref):
    act = (jax.nn.silu(_dot(x, wg_ref[...])) * _dot(x, wu_ref[...])).astype(BF16)
    return _dot(act, wd_ref[...])


def _ffn_epilogue(h, f, gpost_ref, p_ref, wple_ref, wpg_ref):
    h2 = h + _rms(f, gpost_ref[...])
    emb = _dot(p_ref[...].astype(BF16), wple_ref[...])
    return h2 + emb * jax.nn.sigmoid(_dot(h2.astype(BF16), wpg_ref[...]))


def _ffn_kernel(h_ref, gpre_ref, wg_ref, wu_ref, wd_ref, gpost_ref, p_ref, wple_ref, wpg_ref,
                out_ref, hn_ref, acc_ref):
    f = pl.program_id(1)

    @pl.when(f == 0)
    def _():
        hn_ref[...] = _rms(h_ref[...], gpre_ref[...]).astype(BF16)
        acc_ref[...] = jnp.zeros_like(acc_ref)

    acc_ref[...] += _swiglu_chunk(hn_ref[...], wg_ref, wu_ref, wd_ref)

    @pl.when(f == pl.num_programs(1) - 1)
    def _():
        out_ref[...] = _ffn_epilogue(h_ref[...], acc_ref[...], gpost_ref, p_ref, wple_ref, wpg_ref)


def _ffn(h, gpre, wg, wu, wd, gpost, p, w_ple, w_ple_gate, tm):
    n = h.shape[0]
    ff = wg.shape[1]
    row = lambda width: pl.BlockSpec((tm, width), lambda i, f: (i, 0))
    const = lambda shape: pl.BlockSpec(shape, lambda i, f: (0,) * len(shape))
    return pl.pallas_call(
        _ffn_kernel,
        grid=(n // tm, ff // FF_CHUNK),
        in_specs=[row(D_MODEL), const((1, D_MODEL)),
                  pl.BlockSpec((D_MODEL, FF_CHUNK), lambda i, f: (0, f)),
                  pl.BlockSpec((D_MODEL, FF_CHUNK), lambda i, f: (0, f)),
                  pl.BlockSpec((FF_CHUNK, D_MODEL), lambda i, f: (f, 0)),
                  const((1, D_MODEL)), row(PLE_DIM), const((PLE_DIM, D_MODEL)), const((D_MODEL, D_MODEL))],
        out_specs=row(D_MODEL),
        out_shape=jax.ShapeDtypeStruct((n, D_MODEL), F32),
        scratch_shapes=[pltpu.VMEM((tm, D_MODEL), BF16), pltpu.VMEM((tm, D_MODEL), F32)],
        compiler_params=_params(2),
    )(h, gpre, wg, wu, wd, gpost, p, w_ple, w_ple_gate)


SUBLANES = 8
ROUTE_TILE = 512
EXPERT_ROWS = 512
REC_E1, REC_E2, REC_POS1, REC_POS2, REC_W1, REC_W2 = range(6)


def _sorted_rows(tm):
    return 2 * tm + LANES


def _route_kernel(h_ref, gpre_ref, wr_ref, hn_ref, rec_ref, cnt_ref):
    tm = h_ref.shape[0]
    hn = _rms(h_ref[...], gpre_ref[...]).astype(BF16)
    hn_ref[...] = hn
    logits = _dot(hn, wr_ref[...])
    lane = lax.broadcasted_iota(jnp.int32, logits.shape, 1).astype(F32)
    logits = jnp.where(lane < N_EXPERTS, logits, -jnp.inf)
    m1 = jnp.max(logits, axis=-1, keepdims=True)
    i1 = jnp.min(jnp.where(logits == m1, lane, float(LANES)), axis=-1, keepdims=True)
    rest = jnp.where(lane == i1, -jnp.inf, logits)
    m2 = jnp.max(rest, axis=-1, keepdims=True)
    i2 = jnp.min(jnp.where(rest == m2, lane, float(LANES)), axis=-1, keepdims=True)
    t = jnp.exp(m2 - m1)
    den = 1.0 + t

    sel = jnp.where(jnp.logical_or(lane == i1, lane == i2), 1.0, 0.0)
    r = lax.broadcasted_iota(jnp.int32, (tm, tm), 0)
    c = lax.broadcasted_iota(jnp.int32, (tm, tm), 1)
    rank = _dot(jnp.where(c < r, 1.0, 0.0).astype(BF16), sel.astype(BF16))
    cnt = jnp.sum(sel, axis=0, keepdims=True)
    padded = jnp.ceil(cnt * (1.0 / SUBLANES)) * SUBLANES
    er = lax.broadcasted_iota(jnp.int32, (LANES, LANES), 0)
    ec = lax.broadcasted_iota(jnp.int32, (LANES, LANES), 1)
    seg_start = _dot(jnp.broadcast_to(padded, (SUBLANES, LANES)).astype(BF16),
                     jnp.where(er < ec, 1.0, 0.0).astype(BF16))[0:1]
    pos = seg_start + rank
    pos1 = jnp.sum(jnp.where(lane == i1, pos, 0.0), axis=-1, keepdims=True)
    pos2 = jnp.sum(jnp.where(lane == i2, pos, 0.0), axis=-1, keepdims=True)
    rec = jnp.zeros_like(logits)
    for idx, val in ((REC_E1, i1), (REC_E2, i2), (REC_POS1, pos1), (REC_POS2, pos2),
                     (REC_W1, 1.0 / den), (REC_W2, t / den)):
        rec = jnp.where(lane == idx, val, rec)
    rec_ref[...] = rec
    cnt_ref[...] = cnt


def _route(h, gpre, w_router, tm):
    n = h.shape[0]
    row = lambda width: pl.BlockSpec((tm, width), lambda i: (i, 0))
    return pl.pallas_call(
        _route_kernel,
        grid=(n // tm,),
        in_specs=[row(D_MODEL), _const_spec((1, D_MODEL)), _const_spec((D_MODEL, LANES))],
        out_specs=[row(D_MODEL), row(LANES), pl.BlockSpec((None, 1, LANES), lambda i: (i, 0, 0))],
        out_shape=[jax.ShapeDtypeStruct((n, D_MODEL), BF16),
                   jax.ShapeDtypeStruct((n, LANES), F32),
                   jax.ShapeDtypeStruct((n // tm, 1, LANES), F32)],
        compiler_params=_params(1),
    )(h, gpre, w_router)


def _group_copy(src_ref, dst_ref, src_row, dst_row, sem):
    return pltpu.make_async_copy(src_ref.at[pl.ds(src_row, SUBLANES)], dst_ref.at[pl.ds(dst_row, SUBLANES)], sem)


def _for_each_group(tile, pc_ref, lb_ref, base_ref, fn):
    for e in range(N_EXPERTS):
        idx = tile * N_EXPERTS + e
        local0 = lb_ref[idx]
        global0 = base_ref[idx]

        def body(g, carry):
            fn(pl.multiple_of(local0 + g * SUBLANES, SUBLANES), pl.multiple_of(global0 + g * SUBLANES, SUBLANES))
            return carry

        lax.fori_loop(0, pc_ref[idx] // SUBLANES, body, 0)


def _tile_groups(tile, pc_ref, lb_ref):
    last = tile * N_EXPERTS + N_EXPERTS - 1
    return (lb_ref[last] + pc_ref[last]) // SUBLANES


def _dispatch_kernel(pc_ref, lb_ref, base_ref, hn_ref, rec_ref, xs_in_ref, xs_ref, tile_ref, sem):
    del xs_in_ref
    tile = pl.program_id(0)
    tm = hn_ref.shape[0]
    rec_t = rec_ref[...].T
    row = lax.broadcasted_iota(jnp.int32, (_sorted_rows(tm), tm), 0).astype(F32)
    hit = jnp.logical_or(row == rec_t[REC_POS1:REC_POS1 + 1], row == rec_t[REC_POS2:REC_POS2 + 1])
    tile_ref[...] = _dot(jnp.where(hit, 1.0, 0.0).astype(BF16), hn_ref[...])
    _for_each_group(tile, pc_ref, lb_ref, base_ref,
                    lambda lo, gl: _group_copy(tile_ref, xs_ref, lo, gl, sem).start())
    lax.fori_loop(0, _tile_groups(tile, pc_ref, lb_ref),
                  lambda g, c: (_group_copy(tile_ref, xs_ref, 0, 0, sem).wait(), c)[1], 0)


def _dispatch(pc, lb, base, hn, rec, xs_zero, tm):
    n = hn.shape[0]
    row = lambda width: pl.BlockSpec((tm, width), lambda i, *_: (i, 0))
    return pl.pallas_call(
        _dispatch_kernel,
        grid_spec=pltpu.PrefetchScalarGridSpec(
            num_scalar_prefetch=3,
            grid=(n // tm,),
            in_specs=[row(D_MODEL), row(LANES), pl.BlockSpec(memory_space=pl.ANY)],
            out_specs=pl.BlockSpec(memory_space=pl.ANY),
            scratch_shapes=[pltpu.VMEM((_sorted_rows(tm), D_MODEL), F32), pltpu.SemaphoreType.DMA(())]),
        out_shape=jax.ShapeDtypeStruct(xs_zero.shape, F32),
        input_output_aliases={5: 0},
        compiler_params=_params(1),
    )(pc, lb, base, hn, rec, xs_zero)


def _experts_kernel(eid_ref, nused_ref, x_ref, wg_ref, wu_ref, wd_ref, y_ref, xb_ref):
    del eid_ref
    f = pl.program_id(1)
    used = pl.program_id(0) < nused_ref[0]

    @pl.when(f == 0)
    def _():
        y_ref[...] = jnp.zeros_like(y_ref)

    @pl.when(used)
    def _():
        @pl.when(f == 0)
        def _():
            xb_ref[...] = x_ref[...].astype(BF16)

        y_ref[...] += _swiglu_chunk(xb_ref[...], wg_ref, wu_ref, wd_ref)


def _experts(eid, n_used, xs, wg, wu, wd):
    rows = xs.shape[0]
    ff = wg.shape[2]
    n_f = ff // FF_CHUNK
    tile = lambda r, nu: jnp.minimum(r, nu[0] - 1)
    chunk = lambda r, f, nu: jnp.where(r < nu[0], f, n_f - 1)
    return pl.pallas_call(
        _experts_kernel,
        grid_spec=pltpu.PrefetchScalarGridSpec(
            num_scalar_prefetch=2,
            grid=(rows // EXPERT_ROWS, n_f),
            in_specs=[pl.BlockSpec((EXPERT_ROWS, D_MODEL), lambda r, f, eid, nu: (tile(r, nu), 0)),
                      pl.BlockSpec((None, D_MODEL, FF_CHUNK), lambda r, f, eid, nu: (eid[tile(r, nu)], 0, chunk(r, f, nu))),
                      pl.BlockSpec((None, D_MODEL, FF_CHUNK), lambda r, f, eid, nu: (eid[tile(r, nu)], 0, chunk(r, f, nu))),
                      pl.BlockSpec((None, FF_CHUNK, D_MODEL), lambda r, f, eid, nu: (eid[tile(r, nu)], chunk(r, f, nu), 0))],
            out_specs=pl.BlockSpec((EXPERT_ROWS, D_MODEL), lambda r, f, eid, nu: (r, 0)),
            scratch_shapes=[pltpu.VMEM((EXPERT_ROWS, D_MODEL), BF16)]),
        out_shape=jax.ShapeDtypeStruct((rows, D_MODEL), F32),
        compiler_params=_params(2),
    )(eid, n_used, xs, wg, wu, wd)


def _combine_kernel(pc_ref, lb_ref, base_ref, rec_ref, h_ref, gpost_ref, p_ref, wple_ref, wpg_ref, ys_ref,
                    out_ref, tile_ref, sem):
    tile = pl.program_id(0)
    tm = h_ref.shape[0]
    rows = _sorted_rows(tm)
    _for_each_group(tile, pc_ref, lb_ref, base_ref,
                    lambda lo, gl: _group_copy(ys_ref, tile_ref, gl, lo, sem).start())
    rec = rec_ref[...]
    col = lax.broadcasted_iota(jnp.int32, (tm, rows), 1).astype(F32)
    weights = (jnp.where(col == rec[:, REC_POS1:REC_POS1 + 1], rec[:, REC_W1:REC_W1 + 1], 0.0)
               + jnp.where(col == rec[:, REC_POS2:REC_POS2 + 1], rec[:, REC_W2:REC_W2 + 1], 0.0))
    n_groups = _tile_groups(tile, pc_ref, lb_ref)
    lax.fori_loop(0, n_groups, lambda g, c: (_group_copy(ys_ref, tile_ref, 0, 0, sem).wait(), c)[1], 0)
    live = lax.broadcasted_iota(jnp.int32, (rows, 1), 0) < n_groups * SUBLANES
    y = jnp.where(live, tile_ref[...], 0.0).astype(BF16)
    f = _dot(weights.astype(BF16), y)
    out_ref[...] = _ffn_epilogue(h_ref[...], f, gpost_ref, p_ref, wple_ref, wpg_ref)


def _combine(pc, lb, base, rec, h, gpost, p, w_ple, w_ple_gate, ys, tm):
    n = h.shape[0]
    row = lambda width: pl.BlockSpec((tm, width), lambda i, *_: (i, 0))
    const = lambda shape: pl.BlockSpec(shape, lambda i, *_: (0,) * len(shape))
    return pl.pallas_call(
        _combine_kernel,
        grid_spec=pltpu.PrefetchScalarGridSpec(
            num_scalar_prefetch=3,
            grid=(n // tm,),
            in_specs=[row(LANES), row(D_MODEL), const((1, D_MODEL)), row(PLE_DIM), const((PLE_DIM, D_MODEL)),
                      const((D_MODEL, D_MODEL)), pl.BlockSpec(memory_space=pl.ANY)],
            out_specs=row(D_MODEL),
            scratch_shapes=[pltpu.VMEM((_sorted_rows(tm), D_MODEL), F32), pltpu.SemaphoreType.DMA(())]),
        out_shape=jax.ShapeDtypeStruct((n, D_MODEL), F32),
        compiler_params=_params(1),
    )(pc, lb, base, rec, h, gpost, p, w_ple, w_ple_gate, ys)


def _moe(h, gpre, w_router, wg, wu, wd, gpost, p, w_ple, w_ple_gate):
    n = h.shape[0]
    tm = _row_tile(n, ROUTE_TILE)
    n_tiles = n // tm
    hn, rec, cnt = _route(h, gpre, w_router, tm)

    cnt = cnt[:, 0, :N_EXPERTS].astype(jnp.int32)
    pc = (cnt + SUBLANES - 1) // SUBLANES * SUBLANES
    lb = jnp.cumsum(pc, axis=1) - pc
    region = (jnp.sum(pc, axis=0) + EXPERT_ROWS - 1) // EXPERT_ROWS * EXPERT_ROWS
    region_end = jnp.cumsum(region)
    base = (region_end - region)[None, :] + jnp.cumsum(pc, axis=0) - pc
    max_rows = 2 * n + n_tiles * N_EXPERTS * (SUBLANES - 1) + N_EXPERTS * (EXPERT_ROWS - 1)
    max_rows = (max_rows + EXPERT_ROWS - 1) // EXPERT_ROWS * EXPERT_ROWS
    tile_start = jnp.arange(max_rows // EXPERT_ROWS, dtype=jnp.int32) * EXPERT_ROWS
    eid = jnp.minimum(jnp.sum(tile_start[:, None] >= region_end[None, :], axis=1), N_EXPERTS - 1).astype(jnp.int32)
    n_used = (region_end[-1:] // EXPERT_ROWS).astype(jnp.int32)
    flat = lambda a: a.reshape(-1).astype(jnp.int32)

    xs = _dispatch(flat(pc), flat(lb), flat(base), hn, rec, jnp.zeros((max_rows, D_MODEL), F32), tm)
    ys = _experts(eid, n_used, xs, wg, wu, wd)
    return _combine(flat(pc), flat(lb), flat(base), rec, h, gpost, p, w_ple, w_ple_gate, ys, tm)


def _row_tile(n, target):
    tm = min(n, target)
    assert n % tm == 0, (n, tm)
    return tm


def _trunk(x, p, caches, prm, pos0):
    b, t, _ = x.shape
    n = b * t
    depth = prm['w_in'].shape[0]
    h = x.reshape(n, D_MODEL)
    vec = lambda a: a.reshape(1, -1)
    ks, vs, us = [], [], []

    if caches is None:
        cfg = MixerCfg(lq=2 * CHUNK, n_units=4, win_stride=WINDOW, n_seg=1, seg_len=8 * CHUNK,
                       has_halo=True, pos0=pos0)
        assert t % cfg.tb == 0
        bias = _alibi_bias(cfg.lq, WINDOW, lambda r, j: (
            (j // CHUNK >= r // CHUNK) & (j // CHUNK <= r // CHUNK + 2),
            (j // CHUNK >= r // CHUNK) & (j // CHUNK <= r // CHUNK + 2) & (j >= WINDOW)))
    else:
        seqs = 8
        cfg = MixerCfg(lq=t, n_units=seqs, win_stride=KEY_WIN, n_seg=seqs, seg_len=t,
                       has_halo=False, pos0=pos0)
        assert b % seqs == 0 and WINDOW + t <= KEY_WIN
        bias = _alibi_bias(cfg.lq, WINDOW, lambda r, j: ((j < WINDOW + t) & (r >= 0),))

    for i in range(depth):
        q, k, v, u, gates = _in_proj(h, vec(prm['g_mix_pre'][i]), prm['w_in'][i].astype(BF16),
                                     vec(prm['b_in'][i]), _row_tile(n, 512))
        weights = (prm['w_attn_up'][i].astype(BF16), prm['w_pool_group'][i].astype(BF16),
                   vec(prm['pool_scale'][i]), prm['w_pool_up'][i].astype(BF16),
                   prm['w_out'][i].astype(BF16), vec(prm['g_mix_post'][i]))
        sinks = prm['attn_sinks'][i]
        k3 = k.reshape(b, t, KV_WIDTH)
        v3 = v.reshape(b, t, KV_WIDTH)
        u3 = u.reshape(b, t, POOL_WIDTH)
        if caches is None:
            h = _mixer_prompt(cfg, sinks, q.reshape(b, t, ATTN_WIDTH), k3, v3, u3,
                              gates.reshape(b, t, GATE_WIDTH), h.reshape(b, t, D_MODEL),
                              bias, weights).reshape(n, D_MODEL)
            k_all, v_all, u_all = k3, v3, u3
        else:
            k_all = jnp.concatenate([caches[0][i].reshape(b, WINDOW, KV_WIDTH), k3], axis=1)
            v_all = jnp.concatenate([caches[1][i].reshape(b, WINDOW, KV_WIDTH), v3], axis=1)
            u_all = jnp.concatenate([caches[2][i], u3], axis=1)
            pad = ((0, 0), (0, KEY_WIN - WINDOW - t), (0, 0))
            k_win = jnp.pad(k_all, pad).reshape(b * KEY_WIN, KV_WIDTH)
            v_win = jnp.pad(v_all, pad).reshape(b * KEY_WIN, KV_WIDTH)
            u_hist = jnp.pad(caches[2][i], ((0, 0), (HIST_ROWS - POOL_HIST, 0), (0, 0)))
            h = _mixer_sample(cfg, sinks, q, k_win, v_win, u_hist, u, gates, h, bias, weights)
        ks.append(k_all[:, -WINDOW:].reshape(b, WINDOW, N_KV_HEADS, HEAD_DIM))
        vs.append(v_all[:, -WINDOW:].reshape(b, WINDOW, N_KV_HEADS, HEAD_DIM))
        us.append(u_all[:, -POOL_HIST:])

        j = i // 2
        tail = (vec(prm['g_ffn_post'][i]), p[i].reshape(n, PLE_DIM), prm['w_ple'][i].astype(BF16),
                prm['w_ple_gate'][i].astype(BF16))
        if i % 2 == 0:
            h = _ffn(h, vec(prm['g_ffn_pre'][i]), prm['w_gate_dense'][j].astype(BF16),
                     prm['w_up_dense'][j].astype(BF16), prm['w_down_dense'][j].astype(BF16), *tail,
                     _row_tile(n, 512))
        else:
            router = jnp.pad(prm['w_router'][j], ((0, 0), (0, LANES - N_EXPERTS))).astype(BF16)
            h = _moe(h, vec(prm['g_ffn_pre'][i]), router, prm['w_gate_moe'][j].astype(BF16),
                     prm['w_up_moe'][j].astype(BF16), prm['w_down_moe'][j].astype(BF16), *tail)
    return h.reshape(b, t, D_MODEL), jnp.stack(ks), jnp.stack(vs), jnp.stack(us)


def kernel(x_prompt, x_sample, cache_k, cache_v, state_pool, p_prompt, p_sample, w_in, b_in, attn_sinks, w_attn_up, w_pool_group, pool_scale, w_pool_up, w_out, g_mix_pre, g_mix_post, g_ffn_pre, g_ffn_post, w_gate_dense, w_up_dense, w_down_dense, w_router, w_gate_moe, w_up_moe, w_down_moe, w_ple, w_ple_gate):
    prm = {
        'w_in': w_in, 'b_in': b_in, 'attn_sinks': attn_sinks, 'w_attn_up': w_attn_up,
        'w_pool_group': w_pool_group, 'pool_scale': pool_scale, 'w_pool_up': w_pool_up,
        'w_out': w_out, 'g_mix_pre': g_mix_pre, 'g_mix_post': g_mix_post,
        'g_ffn_pre': g_ffn_pre, 'g_ffn_post': g_ffn_post,
        'w_gate_dense': w_gate_dense, 'w_up_dense': w_up_dense, 'w_down_dense': w_down_dense,
        'w_router': w_router, 'w_gate_moe': w_gate_moe, 'w_up_moe': w_up_moe,
        'w_down_moe': w_down_moe, 'w_ple': w_ple, 'w_ple_gate': w_ple_gate,
    }
    y_prompt, k_prompt, v_prompt, pool_prompt = _trunk(x_prompt, p_prompt, None, prm, 0)
    y_sample, k_sample, v_sample, pool_sample = _trunk(
        x_sample, p_sample, (cache_k, cache_v, state_pool), prm, PAST_LEN)
    return (y_prompt, y_sample, k_prompt, v_prompt, pool_prompt, k_sample, v_sample, pool_sample)
```

```python
import functools
from typing import NamedTuple

import jax
import jax.numpy as jnp
from jax import lax
from jax.experimental import pallas as pl
from jax.experimental.pallas import tpu as pltpu

F32 = jnp.float32
BF16 = jnp.bfloat16

D_MODEL = 1024
CHUNK = 64
WINDOW = 128
N_HEADS = 16
N_KV_HEADS = 2
GROUP = N_HEADS // N_KV_HEADS
HEAD_DIM = 64
ATTN_WIDTH = N_HEADS * HEAD_DIM
KV_WIDTH = N_KV_HEADS * HEAD_DIM
POOL_WINDOWS = (2, 4, 8, 16)
POOL_GROUP_DIM = 128
POOL_WIDTH = len(POOL_WINDOWS) * POOL_GROUP_DIM
POOL_HIST = max(POOL_WINDOWS) - 1
GATE_WIDTH = 2 * D_MODEL
IN_WIDTH = ATTN_WIDTH + 2 * KV_WIDTH + POOL_WIDTH + GATE_WIDTH
N_EXPERTS = 8
PLE_DIM = 256
PAST_LEN = 2048
EPS = 1e-6
NEG_INF = -1e30

LANES = 128
V7X_VMEM_BYTES = 64 * 1024 * 1024
VMEM_LIMIT = V7X_VMEM_BYTES * 7 // 8

HEAD_PAIR = 2 * HEAD_DIM
PAIRS_PER_KV = GROUP // 2
KEY_WIN = 2 * WINDOW
HIST_ROWS = POOL_HIST + 1
COL_CHUNK = 512
FF_CHUNK = 512


def _rms(x, g):
    return x * lax.rsqrt(jnp.mean(x * x, axis=-1, keepdims=True) + EPS) * g


def _dot(a, b):
    return jnp.dot(a, b, preferred_element_type=F32)


def _params(n_grid):
    return pltpu.CompilerParams(dimension_semantics=("arbitrary",) * n_grid,
                                vmem_limit_bytes=VMEM_LIMIT)


def _const_spec(shape):
    return pl.BlockSpec(shape, lambda *_: (0,) * len(shape))


def _in_proj_kernel(h_ref, g_ref, w_ref, b_ref, q_ref, k_ref, v_ref, u_ref, gate_ref):
    xn = _rms(h_ref[...], g_ref[...]).astype(BF16)

    def proj(lo, hi):
        return _dot(xn, w_ref[:, lo:hi]) + b_ref[:, lo:hi]

    c1 = ATTN_WIDTH
    c2 = c1 + KV_WIDTH
    c3 = c2 + KV_WIDTH
    c4 = c3 + POOL_WIDTH
    for lo in range(0, c1, COL_CHUNK):
        q_ref[:, lo:lo + COL_CHUNK] = (proj(lo, lo + COL_CHUNK) * (HEAD_DIM ** -0.5)).astype(BF16)
    kv = proj(c1, c3)
    k_ref[...] = kv[:, :KV_WIDTH]
    v_ref[...] = kv[:, KV_WIDTH:]
    u_ref[...] = proj(c3, c4)
    for lo in range(0, GATE_WIDTH, COL_CHUNK):
        gate_ref[:, lo:lo + COL_CHUNK] = jax.nn.sigmoid(proj(c4 + lo, c4 + lo + COL_CHUNK))


def _in_proj(h, g, w, b, tm):
    n = h.shape[0]
    row = lambda width: pl.BlockSpec((tm, width), lambda i: (i, 0))
    return pl.pallas_call(
        _in_proj_kernel,
        grid=(n // tm,),
        in_specs=[row(D_MODEL), _const_spec((1, D_MODEL)), _const_spec((D_MODEL, IN_WIDTH)),
                  _const_spec((1, IN_WIDTH))],
        out_specs=[row(ATTN_WIDTH), row(KV_WIDTH), row(KV_WIDTH), row(POOL_WIDTH), row(GATE_WIDTH)],
        out_shape=[jax.ShapeDtypeStruct((n, ATTN_WIDTH), BF16),
                   jax.ShapeDtypeStruct((n, KV_WIDTH), F32),
                   jax.ShapeDtypeStruct((n, KV_WIDTH), F32),
                   jax.ShapeDtypeStruct((n, POOL_WIDTH), F32),
                   jax.ShapeDtypeStruct((n, GATE_WIDTH), F32)],
        compiler_params=_params(1),
    )(h, g, w, b)


class MixerCfg(NamedTuple):
    lq: int
    n_units: int
    win_stride: int
    n_seg: int
    seg_len: int
    has_halo: bool
    pos0: int

    @property
    def tb(self):
        return self.lq * self.n_units

    @property
    def key_rows(self):
        return (self.n_units - 1) * self.win_stride + KEY_WIN


def _fill_block_diag(dst_ref, x):
    lane = lax.broadcasted_iota(jnp.int32, x.shape, 1)
    lo = lane < HEAD_DIM
    xr = pltpu.roll(x, HEAD_DIM, 1)
    zero = jnp.zeros_like(x)
    dst_ref[0, 0] = jnp.where(lo, x, zero).astype(BF16)
    dst_ref[0, 1] = jnp.where(lo, zero, xr).astype(BF16)
    dst_ref[1, 0] = jnp.where(lo, xr, zero).astype(BF16)
    dst_ref[1, 1] = jnp.where(lo, zero, x).astype(BF16)


def _attention(cfg, q_ref, kbd_ref, vbd_ref, bias_ref, sink_ref, attn_ref, first_block):
    lq = cfg.lq

    def unit(n, carry):
        off_q = pl.multiple_of(n * lq, lq)
        off_w = pl.multiple_of(n * cfg.win_stride, LANES)
        bias_idx = jnp.where(jnp.logical_and(first_block, n == 0), 1, 0) if cfg.has_halo else 0
        for kv in range(N_KV_HEADS):
            cols = [LANES * (PAIRS_PER_KV * kv + p) for p in range(PAIRS_PER_KV)]
            q2 = jnp.concatenate([q_ref[pl.ds(off_q, lq), c:c + HEAD_PAIR] for c in cols], axis=0)
            kbd = jnp.concatenate([kbd_ref[kv, 0, pl.ds(off_w, KEY_WIN), :],
                                   kbd_ref[kv, 1, pl.ds(off_w, KEY_WIN), :]], axis=0)
            vbd = jnp.concatenate([vbd_ref[kv, 0, pl.ds(off_w, KEY_WIN), :],
                                   vbd_ref[kv, 1, pl.ds(off_w, KEY_WIN), :]], axis=0)
            s_all = lax.dot_general(q2, kbd, (((1,), (1,)), ((), ())), preferred_element_type=F32)
            probs, dens = [], []
            for p in range(PAIRS_PER_KV):
                pair_e = []
                for j in range(2):
                    head = kv * GROUP + 2 * p + j
                    s = s_all[p * lq:(p + 1) * lq, j * KEY_WIN:(j + 1) * KEY_WIN] + bias_ref[bias_idx, head]
                    sink = sink_ref[head]
                    m = jnp.maximum(jnp.max(s, axis=-1, keepdims=True), sink)
                    e = jnp.exp(s - m)
                    dens.append(jnp.sum(e, axis=-1, keepdims=True) + jnp.exp(sink - m))
                    pair_e.append(e.astype(BF16))
                probs.append(jnp.concatenate(pair_e, axis=1))
            o_all = _dot(jnp.concatenate(probs, axis=0), vbd)
            lane = lax.broadcasted_iota(jnp.int32, (lq, HEAD_PAIR), 1)
            for p in range(PAIRS_PER_KV):
                den = jnp.where(lane < HEAD_DIM, dens[2 * p], dens[2 * p + 1])
                o = o_all[p * lq:(p + 1) * lq] / den
                attn_ref[pl.ds(off_q, lq), cols[p]:cols[p] + HEAD_PAIR] = o.astype(BF16)
        return carry

    lax.fori_loop(0, cfg.n_units, unit, 0)


def _pooling(cfg, ext_ref, wpg_ref, scale_ref, pool_ref, pos_base):
    ln = cfg.seg_len
    row = lax.broadcasted_iota(jnp.int32, (ln, 1), 0)
    for seg in range(cfg.n_seg):
        pos = pos_base + row
        for g, w in enumerate(POOL_WINDOWS):
            lanes = slice(g * POOL_GROUP_DIM, (g + 1) * POOL_GROUP_DIM)
            u = ext_ref[seg, HIST_ROWS:HIST_ROWS + ln, lanes]
            acc = u
            for back in range(1, w):
                acc = acc + ext_ref[seg, HIST_ROWS - back:HIST_ROWS - back + ln, lanes]
            cnt = jnp.minimum(pos + 1, w).astype(F32)
            mixed = (acc / cnt - u).astype(BF16)
            y = _dot(mixed, wpg_ref[g]) * scale_ref[:, lanes]
            pool_ref[seg * ln:(seg + 1) * ln, lanes] = y.astype(BF16)


def _mixer_kernel(cfg, *refs):
    if cfg.has_halo:
        (sink_ref, q_ref, kh_ref, k_ref, vh_ref, v_ref, uh_ref, u_ref, gate_ref, h_ref, bias_ref,
         wau_ref, wpg_ref, scale_ref, wpu_ref, wout_ref, gpost_ref, out_ref,
         kbd_ref, vbd_ref, ext_ref, attn_ref, pool_ref) = refs
    else:
        (sink_ref, q_ref, k_ref, v_ref, uh_ref, u_ref, gate_ref, h_ref, bias_ref,
         wau_ref, wpg_ref, scale_ref, wpu_ref, wout_ref, gpost_ref, out_ref,
         kbd_ref, vbd_ref, ext_ref, attn_ref, pool_ref) = refs

    step = pl.program_id(1) if cfg.has_halo else pl.program_id(0)
    first_block = step == 0

    if cfg.has_halo:
        _fill_block_diag(kbd_ref, jnp.concatenate([kh_ref[...], k_ref[...]], axis=0))
        _fill_block_diag(vbd_ref, jnp.concatenate([vh_ref[...], v_ref[...]], axis=0))
    else:
        _fill_block_diag(kbd_ref, k_ref[...])
        _fill_block_diag(vbd_ref, v_ref[...])
    _attention(cfg, q_ref, kbd_ref, vbd_ref, bias_ref, sink_ref, attn_ref, first_block)

    if cfg.has_halo:
        hist = uh_ref[...]
        ext_ref[0, 0:HIST_ROWS, :] = jnp.where(first_block, jnp.zeros_like(hist), hist)
        ext_ref[0, HIST_ROWS:, :] = u_ref[...]
        pos_base = cfg.pos0 + step * cfg.tb
    else:
        for seg in range(cfg.n_seg):
            ext_ref[seg, 0:HIST_ROWS, :] = uh_ref[seg]
            ext_ref[seg, HIST_ROWS:, :] = u_ref[seg * cfg.seg_len:(seg + 1) * cfg.seg_len, :]
        pos_base = cfg.pos0
    _pooling(cfg, ext_ref, wpg_ref, scale_ref, pool_ref, pos_base)

    a = _dot(attn_ref[...], wau_ref[...])
    p = _dot(pool_ref[...], wpu_ref[...])
    merged = gate_ref[:, :D_MODEL] * a + gate_ref[:, D_MODEL:] * p
    o = _dot(merged.astype(BF16), wout_ref[...])
    out_ref[...] = h_ref[...] + _rms(o, gpost_ref[...])


def _mixer_weights_specs():
    return [_const_spec((ATTN_WIDTH, D_MODEL)),
            _const_spec((len(POOL_WINDOWS), POOL_GROUP_DIM, POOL_GROUP_DIM)),
            _const_spec((1, POOL_WIDTH)),
            _const_spec((POOL_WIDTH, D_MODEL)),
            _const_spec((D_MODEL, D_MODEL)),
            _const_spec((1, D_MODEL))]


def _mixer_scratch(cfg):
    return [pltpu.VMEM((N_KV_HEADS, 2, cfg.key_rows, KV_WIDTH), BF16),
            pltpu.VMEM((N_KV_HEADS, 2, cfg.key_rows, KV_WIDTH), BF16),
            pltpu.VMEM((cfg.n_seg, HIST_ROWS + cfg.seg_len, POOL_WIDTH), F32),
            pltpu.VMEM((cfg.tb, ATTN_WIDTH), BF16),
            pltpu.VMEM((cfg.tb, POOL_WIDTH), BF16)]


def _mixer_prompt(cfg, sinks, q, k, v, u, gates, h, bias, weights):
    b, t = q.shape[:2]
    tb = cfg.tb
    halo_k = tb // WINDOW
    halo_u = tb // HIST_ROWS
    cur = lambda width: pl.BlockSpec((None, tb, width), lambda bi, i: (bi, i, 0))
    prev_k = pl.BlockSpec((None, WINDOW, KV_WIDTH), lambda bi, i: (bi, jnp.maximum(i * halo_k - 1, 0), 0))
    prev_u = pl.BlockSpec((None, HIST_ROWS, POOL_WIDTH), lambda bi, i: (bi, jnp.maximum(i * halo_u - 1, 0), 0))
    return pl.pallas_call(
        functools.partial(_mixer_kernel, cfg),
        grid=(b, t // tb),
        in_specs=[pl.BlockSpec(memory_space=pltpu.SMEM),
                  cur(ATTN_WIDTH), prev_k, cur(KV_WIDTH), prev_k, cur(KV_WIDTH),
                  prev_u, cur(POOL_WIDTH), cur(GATE_WIDTH), cur(D_MODEL),
                  _const_spec(bias.shape)] + _mixer_weights_specs(),
        out_specs=cur(D_MODEL),
        out_shape=jax.ShapeDtypeStruct((b, t, D_MODEL), F32),
        scratch_shapes=_mixer_scratch(cfg),
        compiler_params=_params(2),
    )(sinks, q, k, k, v, v, u, u, gates, h, bias, *weights)


def _mixer_sample(cfg, sinks, q, k_win, v_win, u_hist, u, gates, h, bias, weights):
    n = q.shape[0]
    tb = cfg.tb
    row = lambda width: pl.BlockSpec((tb, width), lambda i: (i, 0))
    return pl.pallas_call(
        functools.partial(_mixer_kernel, cfg),
        grid=(n // tb,),
        in_specs=[pl.BlockSpec(memory_space=pltpu.SMEM),
                  row(ATTN_WIDTH),
                  pl.BlockSpec((cfg.key_rows, KV_WIDTH), lambda i: (i, 0)),
                  pl.BlockSpec((cfg.key_rows, KV_WIDTH), lambda i: (i, 0)),
                  pl.BlockSpec((cfg.n_seg, HIST_ROWS, POOL_WIDTH), lambda i: (i, 0, 0)),
                  row(POOL_WIDTH), row(GATE_WIDTH), row(D_MODEL),
                  _const_spec(bias.shape)] + _mixer_weights_specs(),
        out_specs=row(D_MODEL),
        out_shape=jax.ShapeDtypeStruct((n, D_MODEL), F32),
        scratch_shapes=_mixer_scratch(cfg),
        compiler_params=_params(1),
    )(sinks, q, k_win, v_win, u_hist, u, gates, h, bias, *weights)


def _alibi_bias(lq, q_off, valid_fn):
    slopes = 2.0 ** (-8.0 * jnp.arange(1, N_HEADS + 1, dtype=F32) / N_HEADS)
    r = jnp.arange(lq)[:, None]
    j = jnp.arange(KEY_WIN)[None, :]
    dist = jnp.abs(r + q_off - j).astype(F32)
    tables = []
    for valid in valid_fn(r, j):
        tables.append(jnp.where(valid[None], -(slopes[:, None, None] * dist[None]), NEG_INF))
    return jnp.stack(tables)


def _swiglu_chunk(x, wg_ref, wu_ref, wd_ref):
    half = wg_ref.shape[1] // 2
    starts = (0, half)
    gate_up = [(_dot(x, wg_ref[:, s:s + half]), _dot(x, wu_ref[:, s:s + half])) for s in starts]
    y = None
    for (g, u), s in zip(gate_up, starts):
        part = _dot((jax.nn.silu(g) * u).astype(BF16), wd_ref[s:s + half, :])
        y = part if y is None else y + part
    return y


def _ffn_epilogue(h, f, gpost_ref, p_ref, wple_ref, wpg_ref):
    h2 = h + _rms(f, gpost_ref[...])
    emb = _dot(p_ref[...].astype(BF16), wple_ref[...])
    return h2 + emb * jax.nn.sigmoid(_dot(h2.astype(BF16), wpg_ref[...]))


def _ffn_kernel(h_ref, gpre_ref, wg_ref, wu_ref, wd_ref, gpost_ref, p_ref, wple_ref, wpg_ref,
                out_ref, hn_ref, acc_ref):
    f = pl.program_id(1)

    @pl.when(f == 0)
    def _():
        hn_ref[...] = _rms(h_ref[...], gpre_ref[...]).astype(BF16)
        acc_ref[...] = jnp.zeros_like(acc_ref)

    acc_ref[...] += _swiglu_chunk(hn_ref[...], wg_ref, wu_ref, wd_ref)

    @pl.when(f == pl.num_programs(1) - 1)
    def _():
        out_ref[...] = _ffn_epilogue(h_ref[...], acc_ref[...], gpost_ref, p_ref, wple_ref, wpg_ref)


def _ffn(h, gpre, wg, wu, wd, gpost, p, w_ple, w_ple_gate, tm):
    n = h.shape[0]
    row = lambda width: pl.BlockSpec((tm, width), lambda i, f: (i, 0))
    const = lambda shape: pl.BlockSpec(shape, lambda i, f: (0,) * len(shape))
    return pl.pallas_call(
        _ffn_kernel,
        grid=(n // tm, wg.shape[0]),
        in_specs=[row(D_MODEL), const((1, D_MODEL)),
                  pl.BlockSpec((None, D_MODEL, FF_CHUNK), lambda i, f: (f, 0, 0)),
                  pl.BlockSpec((None, D_MODEL, FF_CHUNK), lambda i, f: (f, 0, 0)),
                  pl.BlockSpec((FF_CHUNK, D_MODEL), lambda i, f: (f, 0)),
                  const((1, D_MODEL)), row(PLE_DIM), const((PLE_DIM, D_MODEL)), const((D_MODEL, D_MODEL))],
        out_specs=row(D_MODEL),
        out_shape=jax.ShapeDtypeStruct((n, D_MODEL), F32),
        scratch_shapes=[pltpu.VMEM((tm, D_MODEL), BF16), pltpu.VMEM((tm, D_MODEL), F32)],
        compiler_params=_params(2),
    )(h, gpre, wg, wu, wd, gpost, p, w_ple, w_ple_gate)


SUBLANES = 8
ROUTE_TILE = 512
EXPERT_ROWS = 1024
ZERO_ROWS = 128
REC_E1, REC_E2, REC_POS1, REC_POS2, REC_W1, REC_W2 = range(6)


def _sorted_rows(tm):
    return 2 * tm + LANES


def _route_kernel(h_ref, gpre_ref, wr_ref, hn_ref, rec_ref, cnt_ref):
    tm = h_ref.shape[0]
    hn = _rms(h_ref[...], gpre_ref[...]).astype(BF16)
    hn_ref[...] = hn
    logits = _dot(hn, wr_ref[...])
    lane = lax.broadcasted_iota(jnp.int32, logits.shape, 1).astype(F32)
    logits = jnp.where(lane < N_EXPERTS, logits, -jnp.inf)
    m1 = jnp.max(logits, axis=-1, keepdims=True)
    i1 = jnp.min(jnp.where(logits == m1, lane, float(LANES)), axis=-1, keepdims=True)
    rest = jnp.where(lane == i1, -jnp.inf, logits)
    m2 = jnp.max(rest, axis=-1, keepdims=True)
    i2 = jnp.min(jnp.where(rest == m2, lane, float(LANES)), axis=-1, keepdims=True)
    t = jnp.exp(m2 - m1)
    den = 1.0 + t

    sel = jnp.where(jnp.logical_or(lane == i1, lane == i2), 1.0, 0.0)
    r = lax.broadcasted_iota(jnp.int32, (tm, tm), 0)
    c = lax.broadcasted_iota(jnp.int32, (tm, tm), 1)
    rank = _dot(jnp.where(c < r, 1.0, 0.0).astype(BF16), sel.astype(BF16))
    cnt = jnp.sum(sel, axis=0, keepdims=True)
    padded = jnp.ceil(cnt * (1.0 / SUBLANES)) * SUBLANES
    er = lax.broadcasted_iota(jnp.int32, (LANES, LANES), 0)
    ec = lax.broadcasted_iota(jnp.int32, (LANES, LANES), 1)
    seg_start = _dot(jnp.broadcast_to(padded, (SUBLANES, LANES)).astype(BF16),
                     jnp.where(er < ec, 1.0, 0.0).astype(BF16))[0:1]
    pos = seg_start + rank
    pos1 = jnp.sum(jnp.where(lane == i1, pos, 0.0), axis=-1, keepdims=True)
    pos2 = jnp.sum(jnp.where(lane == i2, pos, 0.0), axis=-1, keepdims=True)
    rec = jnp.zeros_like(logits)
    for idx, val in ((REC_E1, i1), (REC_E2, i2), (REC_POS1, pos1), (REC_POS2, pos2),
                     (REC_W1, 1.0 / den), (REC_W2, t / den)):
        rec = jnp.where(lane == idx, val, rec)
    rec_ref[...] = rec
    cnt_ref[...] = cnt


def _route(h, gpre, w_router, tm):
    n = h.shape[0]
    row = lambda width: pl.BlockSpec((tm, width), lambda i: (i, 0))
    return pl.pallas_call(
        _route_kernel,
        grid=(n // tm,),
        in_specs=[row(D_MODEL), _const_spec((1, D_MODEL)), _const_spec((D_MODEL, LANES))],
        out_specs=[row(D_MODEL), row(LANES), pl.BlockSpec((None, 1, LANES), lambda i: (i, 0, 0))],
        out_shape=[jax.ShapeDtypeStruct((n, D_MODEL), BF16),
                   jax.ShapeDtypeStruct((n, LANES), F32),
                   jax.ShapeDtypeStruct((n // tm, 1, LANES), F32)],
        compiler_params=_params(1),
    )(h, gpre, w_router)


def _group_copy(src_ref, dst_ref, src_row, dst_row, sem):
    return pltpu.make_async_copy(src_ref.at[pl.ds(src_row, SUBLANES)], dst_ref.at[pl.ds(dst_row, SUBLANES)], sem)


def _for_each_group(tile, pc_ref, lb_ref, base_ref, fn):
    for e in range(N_EXPERTS):
        idx = tile * N_EXPERTS + e
        local0 = lb_ref[idx]
        global0 = base_ref[idx]

        def body(g, carry):
            fn(pl.multiple_of(local0 + g * SUBLANES, SUBLANES), pl.multiple_of(global0 + g * SUBLANES, SUBLANES))
            return carry

        lax.fori_loop(0, pc_ref[idx] // SUBLANES, body, 0)


def _tile_groups(tile, pc_ref, lb_ref):
    last = tile * N_EXPERTS + N_EXPERTS - 1
    return (lb_ref[last] + pc_ref[last]) // SUBLANES


def _wait_groups(n_groups, src_ref, dst_ref, sem):
    lax.fori_loop(0, n_groups, lambda g, c: (_group_copy(src_ref, dst_ref, 0, 0, sem).wait(), c)[1], 0)


def _zero_copy(zero_ref, xs_ref, row, n_rows, sem):
    return pltpu.make_async_copy(zero_ref.at[pl.ds(0, n_rows)], xs_ref.at[pl.ds(row, n_rows)], sem)


def _dispatch_kernel(pc_ref, lb_ref, base_ref, tail_ref, hn_ref, rec_ref, xs_ref, tile_ref, zero_ref, sem, zsem):
    tile = pl.program_id(0)
    n_tiles = pl.num_programs(0)
    slot = tile % 2
    tm = hn_ref.shape[0]
    rec_t = rec_ref[...].T
    row = lax.broadcasted_iota(jnp.int32, (_sorted_rows(tm), tm), 0).astype(F32)
    hit = jnp.logical_or(row == rec_t[REC_POS1:REC_POS1 + 1], row == rec_t[REC_POS2:REC_POS2 + 1])
    tile_ref[slot] = _dot(jnp.where(hit, 1.0, 0.0).astype(BF16), hn_ref[...])
    _for_each_group(tile, pc_ref, lb_ref, base_ref,
                    lambda lo, gl: _group_copy(tile_ref.at[slot], xs_ref, lo, gl, sem.at[slot]).start())

    @pl.when(tile > 0)
    def _():
        _wait_groups(_tile_groups(tile - 1, pc_ref, lb_ref), tile_ref.at[1 - slot], xs_ref, sem.at[1 - slot])

    @pl.when(tile == n_tiles - 1)
    def _():
        _wait_groups(_tile_groups(tile, pc_ref, lb_ref), tile_ref.at[slot], xs_ref, sem.at[slot])
        zero_ref[...] = jnp.zeros_like(zero_ref)
        for e in range(N_EXPERTS):
            start = tail_ref[e]
            lax.fori_loop(0, tail_ref[N_EXPERTS + e], lambda g, c, start=start: (_zero_copy(
                zero_ref, xs_ref, pl.multiple_of(start + g * SUBLANES, SUBLANES), SUBLANES, zsem.at[0]).start(), c)[1], 0)
            lax.fori_loop(0, tail_ref[N_EXPERTS + e], lambda g, c: (_zero_copy(
                zero_ref, xs_ref, 0, SUBLANES, zsem.at[0]).wait(), c)[1], 0)
        used_rows = tail_ref[2 * N_EXPERTS]
        n_blocks = (xs_ref.shape[0] - used_rows) // ZERO_ROWS
        lax.fori_loop(0, n_blocks, lambda g, c: (_zero_copy(
            zero_ref, xs_ref, pl.multiple_of(used_rows + g * ZERO_ROWS, ZERO_ROWS), ZERO_ROWS, zsem.at[1]).start(), c)[1], 0)
        lax.fori_loop(0, n_blocks, lambda g, c: (_zero_copy(
            zero_ref, xs_ref, 0, ZERO_ROWS, zsem.at[1]).wait(), c)[1], 0)


def _dispatch(pc, lb, base, tail, hn, rec, max_rows, tm):
    n = hn.shape[0]
    row = lambda width: pl.BlockSpec((tm, width), lambda i, *_: (i, 0))
    return pl.pallas_call(
        _dispatch_kernel,
        grid_spec=pltpu.PrefetchScalarGridSpec(
            num_scalar_prefetch=4,
            grid=(n // tm,),
            in_specs=[row(D_MODEL), row(LANES)],
            out_specs=pl.BlockSpec(memory_space=pl.ANY),
            scratch_shapes=[pltpu.VMEM((2, _sorted_rows(tm), D_MODEL), F32),
                            pltpu.VMEM((ZERO_ROWS, D_MODEL), F32),
                            pltpu.SemaphoreType.DMA((2,)), pltpu.SemaphoreType.DMA((2,))]),
        out_shape=jax.ShapeDtypeStruct((max_rows, D_MODEL), F32),
        compiler_params=_params(1),
    )(pc, lb, base, tail, hn, rec)


def _experts_kernel(eid_ref, nused_ref, x_ref, wg_ref, wu_ref, wd_ref, y_ref, xb_ref):
    del eid_ref
    f = pl.program_id(1)
    used = pl.program_id(0) < nused_ref[0]

    @pl.when(f == 0)
    def _():
        y_ref[...] = jnp.zeros_like(y_ref)

    @pl.when(used)
    def _():
        @pl.when(f == 0)
        def _():
            xb_ref[...] = x_ref[...].astype(BF16)

        y_ref[...] += _swiglu_chunk(xb_ref[...], wg_ref, wu_ref, wd_ref)


def _experts(eid, n_used, xs, wg, wu, wd):
    rows = xs.shape[0]
    n_f = wg.shape[1]
    tile = lambda r, nu: jnp.minimum(r, nu[0] - 1)
    chunk = lambda r, f, nu: jnp.where(r < nu[0], f, n_f - 1)
    return pl.pallas_call(
        _experts_kernel,
        grid_spec=pltpu.PrefetchScalarGridSpec(
            num_scalar_prefetch=2,
            grid=(rows // EXPERT_ROWS, n_f),
            in_specs=[pl.BlockSpec((EXPERT_ROWS, D_MODEL), lambda r, f, eid, nu: (tile(r, nu), 0)),
                      pl.BlockSpec((None, None, D_MODEL, FF_CHUNK),
                                   lambda r, f, eid, nu: (eid[tile(r, nu)], chunk(r, f, nu), 0, 0)),
                      pl.BlockSpec((None, None, D_MODEL, FF_CHUNK),
                                   lambda r, f, eid, nu: (eid[tile(r, nu)], chunk(r, f, nu), 0, 0)),
                      pl.BlockSpec((None, FF_CHUNK, D_MODEL), lambda r, f, eid, nu: (eid[tile(r, nu)], chunk(r, f, nu), 0))],
            out_specs=pl.BlockSpec((EXPERT_ROWS, D_MODEL), lambda r, f, eid, nu: (r, 0)),
            scratch_shapes=[pltpu.VMEM((EXPERT_ROWS, D_MODEL), BF16)]),
        out_shape=jax.ShapeDtypeStruct((rows, D_MODEL), F32),
        compiler_params=_params(2),
    )(eid, n_used, xs, wg, wu, wd)


def _combine_kernel(pc_ref, lb_ref, base_ref, rec_ref, h_ref, gpost_ref, p_ref, wple_ref, wpg_ref, ys_ref,
                    out_ref, tile_ref, sem):
    tile = pl.program_id(0)
    slot = tile % 2
    tm = h_ref.shape[0]
    rows = _sorted_rows(tm)

    def fetch(t, s):
        _for_each_group(t, pc_ref, lb_ref, base_ref,
                        lambda lo, gl: _group_copy(ys_ref, tile_ref.at[s], gl, lo, sem.at[s]).start())

    @pl.when(tile == 0)
    def _():
        fetch(tile, slot)

    @pl.when(tile + 1 < pl.num_programs(0))
    def _():
        fetch(tile + 1, 1 - slot)

    rec = rec_ref[...]
    col = lax.broadcasted_iota(jnp.int32, (tm, rows), 1).astype(F32)
    weights = (jnp.where(col == rec[:, REC_POS1:REC_POS1 + 1], rec[:, REC_W1:REC_W1 + 1], 0.0)
               + jnp.where(col == rec[:, REC_POS2:REC_POS2 + 1], rec[:, REC_W2:REC_W2 + 1], 0.0))
    n_groups = _tile_groups(tile, pc_ref, lb_ref)
    _wait_groups(n_groups, ys_ref, tile_ref.at[slot], sem.at[slot])
    live = lax.broadcasted_iota(jnp.int32, (rows, 1), 0) < n_groups * SUBLANES
    y = jnp.where(live, tile_ref[slot], 0.0).astype(BF16)
    f = _dot(weights.astype(BF16), y)
    out_ref[...] = _ffn_epilogue(h_ref[...], f, gpost_ref, p_ref, wple_ref, wpg_ref)


def _combine(pc, lb, base, rec, h, gpost, p, w_ple, w_ple_gate, ys, tm):
    n = h.shape[0]
    row = lambda width: pl.BlockSpec((tm, width), lambda i, *_: (i, 0))
    const = lambda shape: pl.BlockSpec(shape, lambda i, *_: (0,) * len(shape))
    return pl.pallas_call(
        _combine_kernel,
        grid_spec=pltpu.PrefetchScalarGridSpec(
            num_scalar_prefetch=3,
            grid=(n // tm,),
            in_specs=[row(LANES), row(D_MODEL), const((1, D_MODEL)), row(PLE_DIM), const((PLE_DIM, D_MODEL)),
                      const((D_MODEL, D_MODEL)), pl.BlockSpec(memory_space=pl.ANY)],
            out_specs=row(D_MODEL),
            scratch_shapes=[pltpu.VMEM((2, _sorted_rows(tm), D_MODEL), F32), pltpu.SemaphoreType.DMA((2,))]),
        out_shape=jax.ShapeDtypeStruct((n, D_MODEL), F32),
        compiler_params=_params(1),
    )(pc, lb, base, rec, h, gpost, p, w_ple, w_ple_gate, ys)


def _moe(h, gpre, w_router, wg, wu, wd, gpost, p, w_ple, w_ple_gate):
    n = h.shape[0]
    tm = _row_tile(n, ROUTE_TILE)
    n_tiles = n // tm
    hn, rec, cnt = _route(h, gpre, w_router, tm)

    cnt = cnt[:, 0, :N_EXPERTS].astype(jnp.int32)
    pc = (cnt + SUBLANES - 1) // SUBLANES * SUBLANES
    lb = jnp.cumsum(pc, axis=1) - pc
    region = (jnp.sum(pc, axis=0) + EXPERT_ROWS - 1) // EXPERT_ROWS * EXPERT_ROWS
    region_end = jnp.cumsum(region)
    base = (region_end - region)[None, :] + jnp.cumsum(pc, axis=0) - pc
    max_rows = 2 * n + n_tiles * N_EXPERTS * (SUBLANES - 1) + N_EXPERTS * (EXPERT_ROWS - 1)
    max_rows = (max_rows + EXPERT_ROWS - 1) // EXPERT_ROWS * EXPERT_ROWS
    tile_start = jnp.arange(max_rows // EXPERT_ROWS, dtype=jnp.int32) * EXPERT_ROWS
    eid = jnp.minimum(jnp.sum(tile_start[:, None] >= region_end[None, :], axis=1), N_EXPERTS - 1).astype(jnp.int32)
    n_used = (region_end[-1:] // EXPERT_ROWS).astype(jnp.int32)
    total = jnp.sum(pc, axis=0)
    tail = jnp.concatenate([region_end - region + total, (region - total) // SUBLANES, region_end[-1:]])
    flat = lambda a: a.reshape(-1).astype(jnp.int32)

    xs = _dispatch(flat(pc), flat(lb), flat(base), flat(tail), hn, rec, max_rows, tm)
    ys = _experts(eid, n_used, xs, wg, wu, wd)
    return _combine(flat(pc), flat(lb), flat(base), rec, h, gpost, p, w_ple, w_ple_gate, ys, tm)


def _row_tile(n, target):
    tm = min(n, target)
    assert n % tm == 0, (n, tm)
    return tm


def _column_chunks(w):
    *lead, d, ff = w.shape
    w = w.astype(BF16).reshape(*lead, d, ff // FF_CHUNK, FF_CHUNK)
    return jnp.swapaxes(w, -3, -2)


def _trunk(x, p, caches, prm, pos0):
    b, t, _ = x.shape
    n = b * t
    depth = prm['w_in'].shape[0]
    h = x.reshape(n, D_MODEL)
    vec = lambda a: a.reshape(1, -1)
    ks, vs, us = [], [], []

    if caches is None:
        cfg = MixerCfg(lq=2 * CHUNK, n_units=4, win_stride=WINDOW, n_seg=1, seg_len=8 * CHUNK,
                       has_halo=True, pos0=pos0)
        assert t % cfg.tb == 0
        bias = _alibi_bias(cfg.lq, WINDOW, lambda r, j: (
            (j // CHUNK >= r // CHUNK) & (j // CHUNK <= r // CHUNK + 2),
            (j // CHUNK >= r // CHUNK) & (j // CHUNK <= r // CHUNK + 2) & (j >= WINDOW)))
    else:
        seqs = 8
        cfg = MixerCfg(lq=t, n_units=seqs, win_stride=KEY_WIN, n_seg=seqs, seg_len=t,
                       has_halo=False, pos0=pos0)
        assert b % seqs == 0 and WINDOW + t <= KEY_WIN
        bias = _alibi_bias(cfg.lq, WINDOW, lambda r, j: ((j < WINDOW + t) & (r >= 0),))

    for i in range(depth):
        q, k, v, u, gates = _in_proj(h, vec(prm['g_mix_pre'][i]), prm['w_in'][i].astype(BF16),
                                     vec(prm['b_in'][i]), _row_tile(n, 512))
        weights = (prm['w_attn_up'][i].astype(BF16), prm['w_pool_group'][i].astype(BF16),
                   vec(prm['pool_scale'][i]), prm['w_pool_up'][i].astype(BF16),
                   prm['w_out'][i].astype(BF16), vec(prm['g_mix_post'][i]))
        sinks = prm['attn_sinks'][i]
        k3 = k.reshape(b, t, KV_WIDTH)
        v3 = v.reshape(b, t, KV_WIDTH)
        u3 = u.reshape(b, t, POOL_WIDTH)
        if caches is None:
            h = _mixer_prompt(cfg, sinks, q.reshape(b, t, ATTN_WIDTH), k3, v3, u3,
                              gates.reshape(b, t, GATE_WIDTH), h.reshape(b, t, D_MODEL),
                              bias, weights).reshape(n, D_MODEL)
            k_all, v_all, u_all = k3, v3, u3
        else:
            k_all = jnp.concatenate([caches[0][i].reshape(b, WINDOW, KV_WIDTH), k3], axis=1)
            v_all = jnp.concatenate([caches[1][i].reshape(b, WINDOW, KV_WIDTH), v3], axis=1)
            u_all = jnp.concatenate([caches[2][i], u3], axis=1)
            pad = ((0, 0), (0, KEY_WIN - WINDOW - t), (0, 0))
            k_win = jnp.pad(k_all, pad).reshape(b * KEY_WIN, KV_WIDTH)
            v_win = jnp.pad(v_all, pad).reshape(b * KEY_WIN, KV_WIDTH)
            u_hist = jnp.pad(caches[2][i], ((0, 0), (HIST_ROWS - POOL_HIST, 0), (0, 0)))
            h = _mixer_sample(cfg, sinks, q, k_win, v_win, u_hist, u, gates, h, bias, weights)
        ks.append(k_all[:, -WINDOW:].reshape(b, WINDOW, N_KV_HEADS, HEAD_DIM))
        vs.append(v_all[:, -WINDOW:].reshape(b, WINDOW, N_KV_HEADS, HEAD_DIM))
        us.append(u_all[:, -POOL_HIST:])

        j = i // 2
        tail = (vec(prm['g_ffn_post'][i]), p[i].reshape(n, PLE_DIM), prm['w_ple'][i].astype(BF16),
                prm['w_ple_gate'][i].astype(BF16))
        if i % 2 == 0:
            h = _ffn(h, vec(prm['g_ffn_pre'][i]), _column_chunks(prm['w_gate_dense'][j]),
                     _column_chunks(prm['w_up_dense'][j]), prm['w_down_dense'][j].astype(BF16), *tail,
                     _row_tile(n, 1024))
        else:
            router = jnp.pad(prm['w_router'][j], ((0, 0), (0, LANES - N_EXPERTS))).astype(BF16)
            h = _moe(h, vec(prm['g_ffn_pre'][i]), router, _column_chunks(prm['w_gate_moe'][j]),
                     _column_chunks(prm['w_up_moe'][j]), prm['w_down_moe'][j].astype(BF16), *tail)
    return h.reshape(b, t, D_MODEL), jnp.stack(ks), jnp.stack(vs), jnp.stack(us)


def kernel(x_prompt, x_sample, cache_k, cache_v, state_pool, p_prompt, p_sample, w_in, b_in, attn_sinks, w_attn_up, w_pool_group, pool_scale, w_pool_up, w_out, g_mix_pre, g_mix_post, g_ffn_pre, g_ffn_post, w_gate_dense, w_up_dense, w_down_dense, w_router, w_gate_moe, w_up_moe, w_down_moe, w_ple, w_ple_gate):
    prm = {
        'w_in': w_in, 'b_in': b_in, 'attn_sinks': attn_sinks, 'w_attn_up': w_attn_up,
        'w_pool_group': w_pool_group, 'pool_scale': pool_scale, 'w_pool_up': w_pool_up,
        'w_out': w_out, 'g_mix_pre': g_mix_pre, 'g_mix_post': g_mix_post,
        'g_ffn_pre': g_ffn_pre, 'g_ffn_post': g_ffn_post,
        'w_gate_dense': w_gate_dense, 'w_up_dense': w_up_dense, 'w_down_dense': w_down_dense,
        'w_router': w_router, 'w_gate_moe': w_gate_moe, 'w_up_moe': w_up_moe,
        'w_down_moe': w_down_moe, 'w_ple': w_ple, 'w_ple_gate': w_ple_gate,
    }
    y_prompt, k_prompt, v_prompt, pool_prompt = _trunk(x_prompt, p_prompt, None, prm, 0)
    y_sample, k_sample, v_sample, pool_sample = _trunk(
        x_sample, p_sample, (cache_k, cache_v, state_pool), prm, PAST_LEN)
    return (y_prompt, y_sample, k_prompt, v_prompt, pool_prompt, k_sample, v_sample, pool_sample)
```

```python
import functools
from typing import NamedTuple

import jax
import jax.numpy as jnp
from jax import lax
from jax.experimental import pallas as pl
from jax.experimental.pallas import tpu as pltpu

F32 = jnp.float32
BF16 = jnp.bfloat16

D_MODEL = 1024
CHUNK = 64
WINDOW = 128
N_HEADS = 16
N_KV_HEADS = 2
GROUP = N_HEADS // N_KV_HEADS
HEAD_DIM = 64
ATTN_WIDTH = N_HEADS * HEAD_DIM
KV_WIDTH = N_KV_HEADS * HEAD_DIM
POOL_WINDOWS = (2, 4, 8, 16)
POOL_GROUP_DIM = 128
POOL_WIDTH = len(POOL_WINDOWS) * POOL_GROUP_DIM
POOL_HIST = max(POOL_WINDOWS) - 1
GATE_WIDTH = 2 * D_MODEL
IN_WIDTH = ATTN_WIDTH + 2 * KV_WIDTH + POOL_WIDTH + GATE_WIDTH
N_EXPERTS = 8
PLE_DIM = 256
PAST_LEN = 2048
EPS = 1e-6
NEG_INF = -1e30
LOG2E = 1.4426950408889634

LANES = 128
V7X_VMEM_BYTES = 64 * 1024 * 1024
VMEM_LIMIT = V7X_VMEM_BYTES * 7 // 8

HEAD_PAIR = 2 * HEAD_DIM
PAIRS_PER_KV = GROUP // 2
KEY_WIN = 2 * WINDOW
HIST_ROWS = POOL_HIST + 1
COL_CHUNK = 512
FF_CHUNK = 512


def _rms(x, g):
    return x * lax.rsqrt(jnp.mean(x * x, axis=-1, keepdims=True) + EPS) * g


def _dot(a, b):
    return jnp.dot(a, b, preferred_element_type=F32)


def _params(n_grid):
    return pltpu.CompilerParams(dimension_semantics=("arbitrary",) * n_grid,
                                vmem_limit_bytes=VMEM_LIMIT)


def _const_spec(shape):
    return pl.BlockSpec(shape, lambda *_: (0,) * len(shape))


def _in_proj_kernel(h_ref, g_ref, w_ref, b_ref, q_ref, k_ref, v_ref, u_ref, gate_ref):
    xn = _rms(h_ref[...], g_ref[...]).astype(BF16)

    def proj(lo, hi):
        return _dot(xn, w_ref[:, lo:hi]) + b_ref[:, lo:hi]

    c1 = ATTN_WIDTH
    c2 = c1 + KV_WIDTH
    c3 = c2 + KV_WIDTH
    c4 = c3 + POOL_WIDTH
    for lo in range(0, c1, COL_CHUNK):
        q_ref[:, lo:lo + COL_CHUNK] = (proj(lo, lo + COL_CHUNK) * (HEAD_DIM ** -0.5 * LOG2E)).astype(BF16)
    kv = proj(c1, c3)
    k_ref[...] = kv[:, :KV_WIDTH]
    v_ref[...] = kv[:, KV_WIDTH:]
    u_ref[...] = proj(c3, c4)
    for lo in range(0, GATE_WIDTH, COL_CHUNK):
        gate_ref[:, lo:lo + COL_CHUNK] = jax.nn.sigmoid(proj(c4 + lo, c4 + lo + COL_CHUNK))


def _in_proj(h, g, w, b, tm):
    n = h.shape[0]
    row = lambda width: pl.BlockSpec((tm, width), lambda i: (i, 0))
    return pl.pallas_call(
        _in_proj_kernel,
        grid=(n // tm,),
        in_specs=[row(D_MODEL), _const_spec((1, D_MODEL)), _const_spec((D_MODEL, IN_WIDTH)),
                  _const_spec((1, IN_WIDTH))],
        out_specs=[row(ATTN_WIDTH), row(KV_WIDTH), row(KV_WIDTH), row(POOL_WIDTH), row(GATE_WIDTH)],
        out_shape=[jax.ShapeDtypeStruct((n, ATTN_WIDTH), BF16),
                   jax.ShapeDtypeStruct((n, KV_WIDTH), F32),
                   jax.ShapeDtypeStruct((n, KV_WIDTH), F32),
                   jax.ShapeDtypeStruct((n, POOL_WIDTH), F32),
                   jax.ShapeDtypeStruct((n, GATE_WIDTH), F32)],
        compiler_params=_params(1),
    )(h, g, w, b)


class MixerCfg(NamedTuple):
    lq: int
    n_units: int
    win_stride: int
    n_seg: int
    seg_len: int
    has_halo: bool
    pos0: int

    @property
    def tb(self):
        return self.lq * self.n_units

    @property
    def key_rows(self):
        return (self.n_units - 1) * self.win_stride + KEY_WIN


def _fill_block_diag(dst_ref, x):
    lane = lax.broadcasted_iota(jnp.int32, x.shape, 1)
    lo = lane < HEAD_DIM
    xr = pltpu.roll(x, HEAD_DIM, 1)
    zero = jnp.zeros_like(x)
    dst_ref[0, 0] = jnp.where(lo, x, zero).astype(BF16)
    dst_ref[0, 1] = jnp.where(lo, zero, xr).astype(BF16)
    dst_ref[1, 0] = jnp.where(lo, xr, zero).astype(BF16)
    dst_ref[1, 1] = jnp.where(lo, zero, x).astype(BF16)


def _attention(cfg, q_ref, kbd_ref, vbd_ref, bias_ref, sink_ref, attn_ref, first_block):
    lq = cfg.lq

    for n in range(cfg.n_units):
        off_q = n * lq
        off_w = n * cfg.win_stride
        bias_idx = jnp.where(first_block, 1, 0) if (cfg.has_halo and n == 0) else 0
        for kv in range(N_KV_HEADS):
            cols = [LANES * (PAIRS_PER_KV * kv + p) for p in range(PAIRS_PER_KV)]
            q2 = jnp.concatenate([q_ref[pl.ds(off_q, lq), c:c + HEAD_PAIR] for c in cols], axis=0)
            kbd = jnp.concatenate([kbd_ref[kv, 0, pl.ds(off_w, KEY_WIN), :],
                                   kbd_ref[kv, 1, pl.ds(off_w, KEY_WIN), :]], axis=0)
            vbd = jnp.concatenate([vbd_ref[kv, 0, pl.ds(off_w, KEY_WIN), :],
                                   vbd_ref[kv, 1, pl.ds(off_w, KEY_WIN), :]], axis=0)
            s_all = lax.dot_general(q2, kbd, (((1,), (1,)), ((), ())), preferred_element_type=F32)
            probs, dens = [], []
            for p in range(PAIRS_PER_KV):
                pair_e = []
                for j in range(2):
                    head = kv * GROUP + 2 * p + j
                    s = s_all[p * lq:(p + 1) * lq, j * KEY_WIN:(j + 1) * KEY_WIN] + bias_ref[bias_idx, head]
                    sink = sink_ref[head] * LOG2E
                    m = jnp.maximum(jnp.max(s, axis=-1, keepdims=True), sink)
                    e = jnp.exp2(s - m)
                    dens.append(jnp.sum(e, axis=-1, keepdims=True) + jnp.exp2(sink - m))
                    pair_e.append(e.astype(BF16))
                probs.append(jnp.concatenate(pair_e, axis=1))
            o_all = _dot(jnp.concatenate(probs, axis=0), vbd)
            lane = lax.broadcasted_iota(jnp.int32, (lq, HEAD_PAIR), 1)
            for p in range(PAIRS_PER_KV):
                den = jnp.where(lane < HEAD_DIM, dens[2 * p], dens[2 * p + 1])
                o = o_all[p * lq:(p + 1) * lq] / den
                attn_ref[pl.ds(off_q, lq), cols[p]:cols[p] + HEAD_PAIR] = o.astype(BF16)


def _pooling(cfg, ext_ref, wpg_ref, scale_ref, pool_ref, pos_base):
    ln = cfg.seg_len
    row = lax.broadcasted_iota(jnp.int32, (ln, 1), 0)
    for seg in range(cfg.n_seg):
        pos = pos_base + row
        for g, w in enumerate(POOL_WINDOWS):
            lanes = slice(g * POOL_GROUP_DIM, (g + 1) * POOL_GROUP_DIM)
            u = ext_ref[seg, HIST_ROWS:HIST_ROWS + ln, lanes]
            acc = u
            for back in range(1, w):
                acc = acc + ext_ref[seg, HIST_ROWS - back:HIST_ROWS - back + ln, lanes]
            cnt = jnp.minimum(pos + 1, w).astype(F32)
            mixed = (acc / cnt - u).astype(BF16)
            y = _dot(mixed, wpg_ref[g]) * scale_ref[:, lanes]
            pool_ref[seg * ln:(seg + 1) * ln, lanes] = y.astype(BF16)


def _mixer_kernel(cfg, *refs):
    if cfg.has_halo:
        (sink_ref, q_ref, kh_ref, k_ref, vh_ref, v_ref, uh_ref, u_ref, gate_ref, h_ref, bias_ref,
         wau_ref, wpg_ref, scale_ref, wpu_ref, wout_ref, gpost_ref, out_ref,
         kbd_ref, vbd_ref, ext_ref, attn_ref, pool_ref) = refs
    else:
        (sink_ref, q_ref, k_ref, v_ref, uh_ref, u_ref, gate_ref, h_ref, bias_ref,
         wau_ref, wpg_ref, scale_ref, wpu_ref, wout_ref, gpost_ref, out_ref,
         kbd_ref, vbd_ref, ext_ref, attn_ref, pool_ref) = refs

    step = pl.program_id(1) if cfg.has_halo else pl.program_id(0)
    first_block = step == 0

    if cfg.has_halo:
        _fill_block_diag(kbd_ref, jnp.concatenate([kh_ref[...], k_ref[...]], axis=0))
        _fill_block_diag(vbd_ref, jnp.concatenate([vh_ref[...], v_ref[...]], axis=0))
    else:
        _fill_block_diag(kbd_ref, k_ref[...])
        _fill_block_diag(vbd_ref, v_ref[...])
    _attention(cfg, q_ref, kbd_ref, vbd_ref, bias_ref, sink_ref, attn_ref, first_block)

    if cfg.has_halo:
        hist = uh_ref[...]
        ext_ref[0, 0:HIST_ROWS, :] = jnp.where(first_block, jnp.zeros_like(hist), hist)
        ext_ref[0, HIST_ROWS:, :] = u_ref[...]
        pos_base = cfg.pos0 + step * cfg.tb
    else:
        for seg in range(cfg.n_seg):
            ext_ref[seg, 0:HIST_ROWS, :] = uh_ref[seg]
            ext_ref[seg, HIST_ROWS:, :] = u_ref[seg * cfg.seg_len:(seg + 1) * cfg.seg_len, :]
        pos_base = cfg.pos0
    _pooling(cfg, ext_ref, wpg_ref, scale_ref, pool_ref, pos_base)

    a = _dot(attn_ref[...], wau_ref[...])
    p = _dot(pool_ref[...], wpu_ref[...])
    merged = gate_ref[:, :D_MODEL] * a + gate_ref[:, D_MODEL:] * p
    o = _dot(merged.astype(BF16), wout_ref[...])
    out_ref[...] = h_ref[...] + _rms(o, gpost_ref[...])


def _mixer_weights_specs():
    return [_const_spec((ATTN_WIDTH, D_MODEL)),
            _const_spec((len(POOL_WINDOWS), POOL_GROUP_DIM, POOL_GROUP_DIM)),
            _const_spec((1, POOL_WIDTH)),
            _const_spec((POOL_WIDTH, D_MODEL)),
            _const_spec((D_MODEL, D_MODEL)),
            _const_spec((1, D_MODEL))]


def _mixer_scratch(cfg):
    return [pltpu.VMEM((N_KV_HEADS, 2, cfg.key_rows, KV_WIDTH), BF16),
            pltpu.VMEM((N_KV_HEADS, 2, cfg.key_rows, KV_WIDTH), BF16),
            pltpu.VMEM((cfg.n_seg, HIST_ROWS + cfg.seg_len, POOL_WIDTH), F32),
            pltpu.VMEM((cfg.tb, ATTN_WIDTH), BF16),
            pltpu.VMEM((cfg.tb, POOL_WIDTH), BF16)]


def _mixer_prompt(cfg, sinks, q, k, v, u, gates, h, bias, weights):
    b, t = q.shape[:2]
    tb = cfg.tb
    halo_k = tb // WINDOW
    halo_u = tb // HIST_ROWS
    cur = lambda width: pl.BlockSpec((None, tb, width), lambda bi, i: (bi, i, 0))
    prev_k = pl.BlockSpec((None, WINDOW, KV_WIDTH), lambda bi, i: (bi, jnp.maximum(i * halo_k - 1, 0), 0))
    prev_u = pl.BlockSpec((None, HIST_ROWS, POOL_WIDTH), lambda bi, i: (bi, jnp.maximum(i * halo_u - 1, 0), 0))
    return pl.pallas_call(
        functools.partial(_mixer_kernel, cfg),
        grid=(b, t // tb),
        in_specs=[pl.BlockSpec(memory_space=pltpu.SMEM),
                  cur(ATTN_WIDTH), prev_k, cur(KV_WIDTH), prev_k, cur(KV_WIDTH),
                  prev_u, cur(POOL_WIDTH), cur(GATE_WIDTH), cur(D_MODEL),
                  _const_spec(bias.shape)] + _mixer_weights_specs(),
        out_specs=cur(D_MODEL),
        out_shape=jax.ShapeDtypeStruct((b, t, D_MODEL), F32),
        scratch_shapes=_mixer_scratch(cfg),
        compiler_params=_params(2),
    )(sinks, q, k, k, v, v, u, u, gates, h, bias, *weights)


def _mixer_sample(cfg, sinks, q, k_win, v_win, u_hist, u, gates, h, bias, weights):
    n = q.shape[0]
    tb = cfg.tb
    row = lambda width: pl.BlockSpec((tb, width), lambda i: (i, 0))
    return pl.pallas_call(
        functools.partial(_mixer_kernel, cfg),
        grid=(n // tb,),
        in_specs=[pl.BlockSpec(memory_space=pltpu.SMEM),
                  row(ATTN_WIDTH),
                  pl.BlockSpec((cfg.key_rows, KV_WIDTH), lambda i: (i, 0)),
                  pl.BlockSpec((cfg.key_rows, KV_WIDTH), lambda i: (i, 0)),
                  pl.BlockSpec((cfg.n_seg, HIST_ROWS, POOL_WIDTH), lambda i: (i, 0, 0)),
                  row(POOL_WIDTH), row(GATE_WIDTH), row(D_MODEL),
                  _const_spec(bias.shape)] + _mixer_weights_specs(),
        out_specs=row(D_MODEL),
        out_shape=jax.ShapeDtypeStruct((n, D_MODEL), F32),
        scratch_shapes=_mixer_scratch(cfg),
        compiler_params=_params(1),
    )(sinks, q, k_win, v_win, u_hist, u, gates, h, bias, *weights)


def _alibi_bias(lq, q_off, valid_fn):
    slopes = 2.0 ** (-8.0 * jnp.arange(1, N_HEADS + 1, dtype=F32) / N_HEADS)
    r = jnp.arange(lq)[:, None]
    j = jnp.arange(KEY_WIN)[None, :]
    dist = jnp.abs(r + q_off - j).astype(F32)
    tables = []
    for valid in valid_fn(r, j):
        tables.append(jnp.where(valid[None], -(slopes[:, None, None] * dist[None]) * LOG2E, NEG_INF))
    return jnp.stack(tables)


def _swiglu_chunk(x, wg_ref, wu_ref, wd_ref):
    half = wg_ref.shape[1] // 2
    starts = (0, half)
    gate_up = [(_dot(x, wg_ref[:, s:s + half]), _dot(x, wu_ref[:, s:s + half])) for s in starts]
    y = None
    for (g, u), s in zip(gate_up, starts):
        part = _dot((jax.nn.silu(g) * u).astype(BF16), wd_ref[s:s + half, :])
        y = part if y is None else y + part
    return y


def _ffn_epilogue(h, f, gpost_ref, p_ref, wple_ref, wpg_ref):
    h2 = h + _rms(f, gpost_ref[...])
    emb = _dot(p_ref[...].astype(BF16), wple_ref[...])
    return h2 + emb * jax.nn.sigmoid(_dot(h2.astype(BF16), wpg_ref[...]))


def _ffn_kernel(h_ref, gpre_ref, wg_ref, wu_ref, wd_ref, gpost_ref, p_ref, wple_ref, wpg_ref,
                out_ref, hn_ref, acc_ref):
    f = pl.program_id(1)

    @pl.when(f == 0)
    def _():
        hn_ref[...] = _rms(h_ref[...], gpre_ref[...]).astype(BF16)
        acc_ref[...] = jnp.zeros_like(acc_ref)

    acc_ref[...] += _swiglu_chunk(hn_ref[...], wg_ref, wu_ref, wd_ref)

    @pl.when(f == pl.num_programs(1) - 1)
    def _():
        out_ref[...] = _ffn_epilogue(h_ref[...], acc_ref[...], gpost_ref, p_ref, wple_ref, wpg_ref)


def _ffn(h, gpre, wg, wu, wd, gpost, p, w_ple, w_ple_gate, tm):
    n = h.shape[0]
    row = lambda width: pl.BlockSpec((tm, width), lambda i, f: (i, 0))
    const = lambda shape: pl.BlockSpec(shape, lambda i, f: (0,) * len(shape))
    return pl.pallas_call(
        _ffn_kernel,
        grid=(n // tm, wg.shape[0]),
        in_specs=[row(D_MODEL), const((1, D_MODEL)),
                  pl.BlockSpec((None, D_MODEL, FF_CHUNK), lambda i, f: (f, 0, 0)),
                  pl.BlockSpec((None, D_MODEL, FF_CHUNK), lambda i, f: (f, 0, 0)),
                  pl.BlockSpec((FF_CHUNK, D_MODEL), lambda i, f: (f, 0)),
                  const((1, D_MODEL)), row(PLE_DIM), const((PLE_DIM, D_MODEL)), const((D_MODEL, D_MODEL))],
        out_specs=row(D_MODEL),
        out_shape=jax.ShapeDtypeStruct((n, D_MODEL), F32),
        scratch_shapes=[pltpu.VMEM((tm, D_MODEL), BF16), pltpu.VMEM((tm, D_MODEL), F32)],
        compiler_params=_params(2),
    )(h, gpre, wg, wu, wd, gpost, p, w_ple, w_ple_gate)


SUBLANES = 8
ROUTE_TILE = 512
EXPERT_ROWS = 1024
EXPERT_ROWS_SMALL = 256
ZERO_ROWS = 128
REC_E1, REC_E2, REC_POS1, REC_POS2, REC_W1, REC_W2 = range(6)


def _sorted_rows(tm):
    return 2 * tm + LANES


def _route_kernel(h_ref, gpre_ref, wr_ref, hn_ref, rec_ref, cnt_ref):
    tm = h_ref.shape[0]
    hn = _rms(h_ref[...], gpre_ref[...]).astype(BF16)
    hn_ref[...] = hn
    logits = _dot(hn, wr_ref[...])
    lane = lax.broadcasted_iota(jnp.int32, logits.shape, 1).astype(F32)
    logits = jnp.where(lane < N_EXPERTS, logits, -jnp.inf)
    m1 = jnp.max(logits, axis=-1, keepdims=True)
    i1 = jnp.min(jnp.where(logits == m1, lane, float(LANES)), axis=-1, keepdims=True)
    rest = jnp.where(lane == i1, -jnp.inf, logits)
    m2 = jnp.max(rest, axis=-1, keepdims=True)
    i2 = jnp.min(jnp.where(rest == m2, lane, float(LANES)), axis=-1, keepdims=True)
    t = jnp.exp(m2 - m1)
    den = 1.0 + t

    sel = jnp.where(jnp.logical_or(lane == i1, lane == i2), 1.0, 0.0)
    r = lax.broadcasted_iota(jnp.int32, (tm, tm), 0)
    c = lax.broadcasted_iota(jnp.int32, (tm, tm), 1)
    rank = _dot(jnp.where(c < r, 1.0, 0.0).astype(BF16), sel.astype(BF16))
    cnt = jnp.sum(sel, axis=0, keepdims=True)
    padded = jnp.ceil(cnt * (1.0 / SUBLANES)) * SUBLANES
    er = lax.broadcasted_iota(jnp.int32, (LANES, LANES), 0)
    ec = lax.broadcasted_iota(jnp.int32, (LANES, LANES), 1)
    seg_start = _dot(jnp.broadcast_to(padded, (SUBLANES, LANES)).astype(BF16),
                     jnp.where(er < ec, 1.0, 0.0).astype(BF16))[0:1]
    pos = seg_start + rank
    pos1 = jnp.sum(jnp.where(lane == i1, pos, 0.0), axis=-1, keepdims=True)
    pos2 = jnp.sum(jnp.where(lane == i2, pos, 0.0), axis=-1, keepdims=True)
    rec = jnp.zeros_like(logits)
    for idx, val in ((REC_E1, i1), (REC_E2, i2), (REC_POS1, pos1), (REC_POS2, pos2),
                     (REC_W1, 1.0 / den), (REC_W2, t / den)):
        rec = jnp.where(lane == idx, val, rec)
    rec_ref[...] = rec
    cnt_ref[...] = cnt


def _route(h, gpre, w_router, tm):
    n = h.shape[0]
    row = lambda width: pl.BlockSpec((tm, width), lambda i: (i, 0))
    return pl.pallas_call(
        _route_kernel,
        grid=(n // tm,),
        in_specs=[row(D_MODEL), _const_spec((1, D_MODEL)), _const_spec((D_MODEL, LANES))],
        out_specs=[row(D_MODEL), row(LANES), pl.BlockSpec((None, 1, LANES), lambda i: (i, 0, 0))],
        out_shape=[jax.ShapeDtypeStruct((n, D_MODEL), BF16),
                   jax.ShapeDtypeStruct((n, LANES), F32),
                   jax.ShapeDtypeStruct((n // tm, 1, LANES), F32)],
        compiler_params=_params(1),
    )(h, gpre, w_router)


def _group_copy(src_ref, dst_ref, src_row, dst_row, sem):
    return pltpu.make_async_copy(src_ref.at[pl.ds(src_row, SUBLANES)], dst_ref.at[pl.ds(dst_row, SUBLANES)], sem)


def _for_each_group(tile, pc_ref, lb_ref, base_ref, fn):
    for e in range(N_EXPERTS):
        idx = tile * N_EXPERTS + e
        local0 = lb_ref[idx]
        global0 = base_ref[idx]

        def body(g, carry):
            fn(pl.multiple_of(local0 + g * SUBLANES, SUBLANES), pl.multiple_of(global0 + g * SUBLANES, SUBLANES))
            return carry

        lax.fori_loop(0, pc_ref[idx] // SUBLANES, body, 0)


def _tile_groups(tile, pc_ref, lb_ref):
    last = tile * N_EXPERTS + N_EXPERTS - 1
    return (lb_ref[last] + pc_ref[last]) // SUBLANES


def _wait_groups(n_groups, src_ref, dst_ref, sem):
    @pl.when(n_groups > 0)
    def _():
        rows = n_groups * SUBLANES
        pltpu.make_async_copy(src_ref.at[pl.ds(0, rows)], dst_ref.at[pl.ds(0, rows)], sem).wait()


def _zero_copy(zero_ref, xs_ref, row, n_rows, sem):
    return pltpu.make_async_copy(zero_ref.at[pl.ds(0, n_rows)], xs_ref.at[pl.ds(row, n_rows)], sem)


def _dispatch_kernel(pc_ref, lb_ref, base_ref, tail_ref, hn_ref, rec_ref, xs_ref, tile_ref, zero_ref, sem, zsem):
    tile = pl.program_id(0)
    n_tiles = pl.num_programs(0)
    slot = tile % 2
    tm = hn_ref.shape[0]
    rec_t = rec_ref[...].T
    row = lax.broadcasted_iota(jnp.int32, (_sorted_rows(tm), tm), 0).astype(F32)
    hit = jnp.logical_or(row == rec_t[REC_POS1:REC_POS1 + 1], row == rec_t[REC_POS2:REC_POS2 + 1])
    tile_ref[slot] = _dot(jnp.where(hit, 1.0, 0.0).astype(BF16), hn_ref[...])
    _for_each_group(tile, pc_ref, lb_ref, base_ref,
                    lambda lo, gl: _group_copy(tile_ref.at[slot], xs_ref, lo, gl, sem.at[slot]).start())

    @pl.when(tile > 0)
    def _():
        _wait_groups(_tile_groups(tile - 1, pc_ref, lb_ref), tile_ref.at[1 - slot], xs_ref, sem.at[1 - slot])

    @pl.when(tile == n_tiles - 1)
    def _():
        _wait_groups(_tile_groups(tile, pc_ref, lb_ref), tile_ref.at[slot], xs_ref, sem.at[slot])
        zero_ref[...] = jnp.zeros_like(zero_ref)
        for e in range(N_EXPERTS):
            start = tail_ref[e]
            lax.fori_loop(0, tail_ref[N_EXPERTS + e], lambda g, c, start=start: (_zero_copy(
                zero_ref, xs_ref, pl.multiple_of(start + g * SUBLANES, SUBLANES), SUBLANES, zsem.at[0]).start(), c)[1], 0)
            lax.fori_loop(0, tail_ref[N_EXPERTS + e], lambda g, c: (_zero_copy(
                zero_ref, xs_ref, 0, SUBLANES, zsem.at[0]).wait(), c)[1], 0)
        used_rows = tail_ref[2 * N_EXPERTS]
        n_blocks = (xs_ref.shape[0] - used_rows) // ZERO_ROWS
        lax.fori_loop(0, n_blocks, lambda g, c: (_zero_copy(
            zero_ref, xs_ref, pl.multiple_of(used_rows + g * ZERO_ROWS, ZERO_ROWS), ZERO_ROWS, zsem.at[1]).start(), c)[1], 0)
        lax.fori_loop(0, n_blocks, lambda g, c: (_zero_copy(
            zero_ref, xs_ref, 0, ZERO_ROWS, zsem.at[1]).wait(), c)[1], 0)


def _dispatch(pc, lb, base, tail, hn, rec, max_rows, tm):
    n = hn.shape[0]
    row = lambda width: pl.BlockSpec((tm, width), lambda i, *_: (i, 0))
    return pl.pallas_call(
        _dispatch_kernel,
        grid_spec=pltpu.PrefetchScalarGridSpec(
            num_scalar_prefetch=4,
            grid=(n // tm,),
            in_specs=[row(D_MODEL), row(LANES)],
            out_specs=pl.BlockSpec(memory_space=pl.ANY),
            scratch_shapes=[pltpu.VMEM((2, _sorted_rows(tm), D_MODEL), F32),
                            pltpu.VMEM((ZERO_ROWS, D_MODEL), F32),
                            pltpu.SemaphoreType.DMA((2,)), pltpu.SemaphoreType.DMA((2,))]),
        out_shape=jax.ShapeDtypeStruct((max_rows, D_MODEL), F32),
        compiler_params=_params(1),
    )(pc, lb, base, tail, hn, rec)


def _experts_kernel(eid_ref, nused_ref, x_ref, wg_ref, wu_ref, wd_ref, y_ref, xb_ref):
    del eid_ref
    f = pl.program_id(1)
    used = pl.program_id(0) < nused_ref[0]

    @pl.when(jnp.logical_and(used, f == 0))
    def _():
        xb_ref[...] = x_ref[...].astype(BF16)
        y_ref[...] = _swiglu_chunk(xb_ref[...], wg_ref, wu_ref, wd_ref)

    @pl.when(jnp.logical_and(used, f > 0))
    def _():
        y_ref[...] += _swiglu_chunk(xb_ref[...], wg_ref, wu_ref, wd_ref)

    @pl.when(jnp.logical_and(jnp.logical_not(used), f == 0))
    def _():
        y_ref[...] = jnp.zeros_like(y_ref)


def _experts(eid, n_used, xs, wg, wu, wd, rt):
    rows = xs.shape[0]
    n_f = wg.shape[1]
    tile = lambda r, nu: jnp.minimum(r, nu[0] - 1)
    chunk = lambda r, f, nu: jnp.where(r < nu[0], f, n_f - 1)
    return pl.pallas_call(
        _experts_kernel,
        grid_spec=pltpu.PrefetchScalarGridSpec(
            num_scalar_prefetch=2,
            grid=(rows // rt, n_f),
            in_specs=[pl.BlockSpec((rt, D_MODEL), lambda r, f, eid, nu: (tile(r, nu), 0)),
                      pl.BlockSpec((None, None, D_MODEL, FF_CHUNK),
                                   lambda r, f, eid, nu: (eid[tile(r, nu)], chunk(r, f, nu), 0, 0)),
                      pl.BlockSpec((None, None, D_MODEL, FF_CHUNK),
                                   lambda r, f, eid, nu: (eid[tile(r, nu)], chunk(r, f, nu), 0, 0)),
                      pl.BlockSpec((None, FF_CHUNK, D_MODEL), lambda r, f, eid, nu: (eid[tile(r, nu)], chunk(r, f, nu), 0))],
            out_specs=pl.BlockSpec((rt, D_MODEL), lambda r, f, eid, nu: (r, 0)),
            scratch_shapes=[pltpu.VMEM((rt, D_MODEL), BF16)]),
        out_shape=jax.ShapeDtypeStruct((rows, D_MODEL), F32),
        compiler_params=_params(2),
    )(eid, n_used, xs, wg, wu, wd)


def _combine_kernel(pc_ref, lb_ref, base_ref, rec_ref, h_ref, gpost_ref, p_ref, wple_ref, wpg_ref, ys_ref,
                    out_ref, tile_ref, sem):
    tile = pl.program_id(0)
    slot = tile % 2
    tm = h_ref.shape[0]
    rows = _sorted_rows(tm)

    def fetch(t, s):
        _for_each_group(t, pc_ref, lb_ref, base_ref,
                        lambda lo, gl: _group_copy(ys_ref, tile_ref.at[s], gl, lo, sem.at[s]).start())

    @pl.when(tile == 0)
    def _():
        fetch(tile, slot)

    @pl.when(tile + 1 < pl.num_programs(0))
    def _():
        fetch(tile + 1, 1 - slot)

    rec = rec_ref[...]
    col = lax.broadcasted_iota(jnp.int32, (tm, rows), 1).astype(F32)
    weights = (jnp.where(col == rec[:, REC_POS1:REC_POS1 + 1], rec[:, REC_W1:REC_W1 + 1], 0.0)
               + jnp.where(col == rec[:, REC_POS2:REC_POS2 + 1], rec[:, REC_W2:REC_W2 + 1], 0.0))
    n_groups = _tile_groups(tile, pc_ref, lb_ref)
    _wait_groups(n_groups, ys_ref, tile_ref.at[slot], sem.at[slot])
    live = lax.broadcasted_iota(jnp.int32, (rows, 1), 0) < n_groups * SUBLANES
    y = jnp.where(live, tile_ref[slot], 0.0).astype(BF16)
    f = _dot(weights.astype(BF16), y)
    out_ref[...] = _ffn_epilogue(h_ref[...], f, gpost_ref, p_ref, wple_ref, wpg_ref)


def _combine(pc, lb, base, rec, h, gpost, p, w_ple, w_ple_gate, ys, tm):
    n = h.shape[0]
    row = lambda width: pl.BlockSpec((tm, width), lambda i, *_: (i, 0))
    const = lambda shape: pl.BlockSpec(shape, lambda i, *_: (0,) * len(shape))
    return pl.pallas_call(
        _combine_kernel,
        grid_spec=pltpu.PrefetchScalarGridSpec(
            num_scalar_prefetch=3,
            grid=(n // tm,),
            in_specs=[row(LANES), row(D_MODEL), const((1, D_MODEL)), row(PLE_DIM), const((PLE_DIM, D_MODEL)),
                      const((D_MODEL, D_MODEL)), pl.BlockSpec(memory_space=pl.ANY)],
            out_specs=row(D_MODEL),
            scratch_shapes=[pltpu.VMEM((2, _sorted_rows(tm), D_MODEL), F32), pltpu.SemaphoreType.DMA((2,))]),
        out_shape=jax.ShapeDtypeStruct((n, D_MODEL), F32),
        compiler_params=_params(1),
    )(pc, lb, base, rec, h, gpost, p, w_ple, w_ple_gate, ys)


def _moe(h, gpre, w_router, wg, wu, wd, gpost, p, w_ple, w_ple_gate):
    n = h.shape[0]
    tm = _row_tile(n, ROUTE_TILE)
    n_tiles = n // tm
    hn, rec, cnt = _route(h, gpre, w_router, tm)
    rt = EXPERT_ROWS if 2 * n >= 4 * N_EXPERTS * EXPERT_ROWS else EXPERT_ROWS_SMALL

    cnt = cnt[:, 0, :N_EXPERTS].astype(jnp.int32)
    pc = (cnt + SUBLANES - 1) // SUBLANES * SUBLANES
    lb = jnp.cumsum(pc, axis=1) - pc
    region = (jnp.sum(pc, axis=0) + rt - 1) // rt * rt
    region_end = jnp.cumsum(region)
    base = (region_end - region)[None, :] + jnp.cumsum(pc, axis=0) - pc
    max_rows = 2 * n + n_tiles * N_EXPERTS * (SUBLANES - 1) + N_EXPERTS * (rt - 1)
    max_rows = (max_rows + rt - 1) // rt * rt
    tile_start = jnp.arange(max_rows // rt, dtype=jnp.int32) * rt
    eid = jnp.minimum(jnp.sum(tile_start[:, None] >= region_end[None, :], axis=1), N_EXPERTS - 1).astype(jnp.int32)
    n_used = (region_end[-1:] // rt).astype(jnp.int32)
    total = jnp.sum(pc, axis=0)
    tail = jnp.concatenate([region_end - region + total, (region - total) // SUBLANES, region_end[-1:]])
    flat = lambda a: a.reshape(-1).astype(jnp.int32)

    xs = _dispatch(flat(pc), flat(lb), flat(base), flat(tail), hn, rec, max_rows, tm)
    ys = _experts(eid, n_used, xs, wg, wu, wd, rt)
    return _combine(flat(pc), flat(lb), flat(base), rec, h, gpost, p, w_ple, w_ple_gate, ys, tm)


def _row_tile(n, target):
    tm = min(n, target)
    assert n % tm == 0, (n, tm)
    return tm


def _column_chunks_kernel(w_ref, o_ref):
    o_ref[...] = w_ref[...].astype(BF16)


def _column_chunks(w):
    e, d, ff = w.shape
    return pl.pallas_call(
        _column_chunks_kernel,
        grid=(e, ff // FF_CHUNK),
        in_specs=[pl.BlockSpec((None, d, FF_CHUNK), lambda i, f: (i, 0, f))],
        out_specs=pl.BlockSpec((None, None, d, FF_CHUNK), lambda i, f: (i, f, 0, 0)),
        out_shape=jax.ShapeDtypeStruct((e, ff // FF_CHUNK, d, FF_CHUNK), BF16),
        compiler_params=_params(2),
    )(w)


def _prepare(prm):
    out = dict(prm)
    for name in ('w_in', 'w_attn_up', 'w_pool_group', 'w_pool_up', 'w_out', 'w_down_dense', 'w_down_moe',
                 'w_ple', 'w_ple_gate'):
        out[name] = prm[name].astype(BF16)
    out['w_router'] = jnp.pad(prm['w_router'], ((0, 0), (0, 0), (0, LANES - N_EXPERTS))).astype(BF16)
    out['w_gate_dense'] = _column_chunks(prm['w_gate_dense'])
    out['w_up_dense'] = _column_chunks(prm['w_up_dense'])
    n_moe, n_exp = prm['w_gate_moe'].shape[:2]
    for name in ('w_gate_moe', 'w_up_moe'):
        w = _column_chunks(prm[name].reshape((n_moe * n_exp,) + prm[name].shape[2:]))
        out[name] = w.reshape((n_moe, n_exp) + w.shape[1:])
    return out


def _trunk(x, p, caches, prm, pos0):
    b, t, _ = x.shape
    n = b * t
    depth = prm['w_in'].shape[0]
    h = x.reshape(n, D_MODEL)
    vec = lambda a: a.reshape(1, -1)
    ks, vs, us = [], [], []

    if caches is None:
        cfg = MixerCfg(lq=2 * CHUNK, n_units=4, win_stride=WINDOW, n_seg=1, seg_len=8 * CHUNK,
                       has_halo=True, pos0=pos0)
        assert t % cfg.tb == 0
        bias = _alibi_bias(cfg.lq, WINDOW, lambda r, j: (
            (j // CHUNK >= r // CHUNK) & (j // CHUNK <= r // CHUNK + 2),
            (j // CHUNK >= r // CHUNK) & (j // CHUNK <= r // CHUNK + 2) & (j >= WINDOW)))
    else:
        seqs = 8
        cfg = MixerCfg(lq=t, n_units=seqs, win_stride=KEY_WIN, n_seg=seqs, seg_len=t,
                       has_halo=False, pos0=pos0)
        assert b % seqs == 0 and WINDOW + t <= KEY_WIN
        bias = _alibi_bias(cfg.lq, WINDOW, lambda r, j: ((j < WINDOW + t) & (r >= 0),))

    for i in range(depth):
        q, k, v, u, gates = _in_proj(h, vec(prm['g_mix_pre'][i]), prm['w_in'][i],
                                     vec(prm['b_in'][i]), _row_tile(n, 512))
        weights = (prm['w_attn_up'][i], prm['w_pool_group'][i], vec(prm['pool_scale'][i]),
                   prm['w_pool_up'][i], prm['w_out'][i], vec(prm['g_mix_post'][i]))
        sinks = prm['attn_sinks'][i]
        k3 = k.reshape(b, t, KV_WIDTH)
        v3 = v.reshape(b, t, KV_WIDTH)
        u3 = u.reshape(b, t, POOL_WIDTH)
        if caches is None:
            h = _mixer_prompt(cfg, sinks, q.reshape(b, t, ATTN_WIDTH), k3, v3, u3,
                              gates.reshape(b, t, GATE_WIDTH), h.reshape(b, t, D_MODEL),
                              bias, weights).reshape(n, D_MODEL)
            k_all, v_all, u_all = k3, v3, u3
        else:
            k_all = jnp.concatenate([caches[0][i].reshape(b, WINDOW, KV_WIDTH), k3], axis=1)
            v_all = jnp.concatenate([caches[1][i].reshape(b, WINDOW, KV_WIDTH), v3], axis=1)
            u_all = jnp.concatenate([caches[2][i], u3], axis=1)
            pad = ((0, 0), (0, KEY_WIN - WINDOW - t), (0, 0))
            k_win = jnp.pad(k_all, pad).reshape(b * KEY_WIN, KV_WIDTH)
            v_win = jnp.pad(v_all, pad).reshape(b * KEY_WIN, KV_WIDTH)
            u_hist = jnp.pad(caches[2][i], ((0, 0), (HIST_ROWS - POOL_HIST, 0), (0, 0)))
            h = _mixer_sample(cfg, sinks, q, k_win, v_win, u_hist, u, gates, h, bias, weights)
        ks.append(k_all[:, -WINDOW:].reshape(b, WINDOW, N_KV_HEADS, HEAD_DIM))
        vs.append(v_all[:, -WINDOW:].reshape(b, WINDOW, N_KV_HEADS, HEAD_DIM))
        us.append(u_all[:, -POOL_HIST:])

        j = i // 2
        tail = (vec(prm['g_ffn_post'][i]), p[i].reshape(n, PLE_DIM), prm['w_ple'][i], prm['w_ple_gate'][i])
        if i % 2 == 0:
            h = _ffn(h, vec(prm['g_ffn_pre'][i]), prm['w_gate_dense'][j], prm['w_up_dense'][j],
                     prm['w_down_dense'][j], *tail, _row_tile(n, 1024))
        else:
            h = _moe(h, vec(prm['g_ffn_pre'][i]), prm['w_router'][j], prm['w_gate_moe'][j],
                     prm['w_up_moe'][j], prm['w_down_moe'][j], *tail)
    return h.reshape(b, t, D_MODEL), jnp.stack(ks), jnp.stack(vs), jnp.stack(us)


def kernel(x_prompt, x_sample, cache_k, cache_v, state_pool, p_prompt, p_sample, w_in, b_in, attn_sinks, w_attn_up, w_pool_group, pool_scale, w_pool_up, w_out, g_mix_pre, g_mix_post, g_ffn_pre, g_ffn_post, w_gate_dense, w_up_dense, w_down_dense, w_router, w_gate_moe, w_up_moe, w_down_moe, w_ple, w_ple_gate):
    prm = {
        'w_in': w_in, 'b_in': b_in, 'attn_sinks': attn_sinks, 'w_attn_up': w_attn_up,
        'w_pool_group': w_pool_group, 'pool_scale': pool_scale, 'w_pool_up': w_pool_up,
        'w_out': w_out, 'g_mix_pre': g_mix_pre, 'g_mix_post': g_mix_post,
        'g_ffn_pre': g_ffn_pre, 'g_ffn_post': g_ffn_post,
        'w_gate_dense': w_gate_dense, 'w_up_dense': w_up_dense, 'w_down_dense': w_down_dense,
        'w_router': w_router, 'w_gate_moe': w_gate_moe, 'w_up_moe': w_up_moe,
        'w_down_moe': w_down_moe, 'w_ple': w_ple, 'w_ple_gate': w_ple_gate,
    }
    prm = _prepare(prm)
    y_prompt, k_prompt, v_prompt, pool_prompt = _trunk(x_prompt, p_prompt, None, prm, 0)
    y_sample, k_sample, v_sample, pool_sample = _trunk(
        x_sample, p_sample, (cache_k, cache_v, state_pool), prm, PAST_LEN)
    return (y_prompt, y_sample, k_prompt, v_prompt, pool_prompt, k_sample, v_sample, pool_sample)
```

```python
import functools
from typing import NamedTuple

import jax
import jax.numpy as jnp
from jax import lax
from jax.experimental import pallas as pl
from jax.experimental.pallas import tpu as pltpu

F32 = jnp.float32
BF16 = jnp.bfloat16

D_MODEL = 1024
CHUNK = 64
WINDOW = 128
N_HEADS = 16
N_KV_HEADS = 2
GROUP = N_HEADS // N_KV_HEADS
HEAD_DIM = 64
ATTN_WIDTH = N_HEADS * HEAD_DIM
KV_WIDTH = N_KV_HEADS * HEAD_DIM
POOL_WINDOWS = (2, 4, 8, 16)
POOL_GROUP_DIM = 128
POOL_WIDTH = len(POOL_WINDOWS) * POOL_GROUP_DIM
POOL_HIST = max(POOL_WINDOWS) - 1
GATE_WIDTH = 2 * D_MODEL
IN_WIDTH = ATTN_WIDTH + 2 * KV_WIDTH + POOL_WIDTH + GATE_WIDTH
N_EXPERTS = 8
PLE_DIM = 256
PAST_LEN = 2048
EPS = 1e-6
NEG_INF = -1e30
LOG2E = 1.4426950408889634

LANES = 128
V7X_VMEM_BYTES = 64 * 1024 * 1024
VMEM_LIMIT = V7X_VMEM_BYTES * 7 // 8

HEAD_PAIR = 2 * HEAD_DIM
PAIRS_PER_KV = GROUP // 2
KEY_WIN = 2 * WINDOW
HIST_ROWS = POOL_HIST + 1
COL_CHUNK = 512
FF_CHUNK_DENSE = 2048
FF_CHUNK_MOE = 1792
FF_SUB = 256


def _rms(x, g):
    return x * lax.rsqrt(jnp.mean(x * x, axis=-1, keepdims=True) + EPS) * g


def _dot(a, b):
    return jnp.dot(a, b, preferred_element_type=F32)


def _params(n_grid):
    return pltpu.CompilerParams(dimension_semantics=("arbitrary",) * n_grid,
                                vmem_limit_bytes=VMEM_LIMIT)


def _const_spec(shape):
    return pl.BlockSpec(shape, lambda *_: (0,) * len(shape))


def _in_proj_kernel(h_ref, g_ref, w_ref, b_ref, q_ref, k_ref, v_ref, u_ref, gate_ref):
    xn = _rms(h_ref[...], g_ref[...]).astype(BF16)

    def proj(lo, hi):
        return _dot(xn, w_ref[:, lo:hi]) + b_ref[:, lo:hi]

    c1 = ATTN_WIDTH
    c2 = c1 + KV_WIDTH
    c3 = c2 + KV_WIDTH
    c4 = c3 + POOL_WIDTH
    for lo in range(0, c1, COL_CHUNK):
        q_ref[:, lo:lo + COL_CHUNK] = (proj(lo, lo + COL_CHUNK) * (HEAD_DIM ** -0.5 * LOG2E)).astype(BF16)
    kv = proj(c1, c3)
    k_ref[...] = kv[:, :KV_WIDTH]
    v_ref[...] = kv[:, KV_WIDTH:]
    u_ref[...] = proj(c3, c4)
    for lo in range(0, GATE_WIDTH, COL_CHUNK):
        gate_ref[:, lo:lo + COL_CHUNK] = jax.nn.sigmoid(proj(c4 + lo, c4 + lo + COL_CHUNK))


def _in_proj(h, g, w, b, tm):
    n = h.shape[0]
    row = lambda width: pl.BlockSpec((tm, width), lambda i: (i, 0))
    return pl.pallas_call(
        _in_proj_kernel,
        grid=(n // tm,),
        in_specs=[row(D_MODEL), _const_spec((1, D_MODEL)), _const_spec((D_MODEL, IN_WIDTH)),
                  _const_spec((1, IN_WIDTH))],
        out_specs=[row(ATTN_WIDTH), row(KV_WIDTH), row(KV_WIDTH), row(POOL_WIDTH), row(GATE_WIDTH)],
        out_shape=[jax.ShapeDtypeStruct((n, ATTN_WIDTH), BF16),
                   jax.ShapeDtypeStruct((n, KV_WIDTH), F32),
                   jax.ShapeDtypeStruct((n, KV_WIDTH), F32),
                   jax.ShapeDtypeStruct((n, POOL_WIDTH), F32),
                   jax.ShapeDtypeStruct((n, GATE_WIDTH), F32)],
        compiler_params=_params(1),
    )(h, g, w, b)


class MixerCfg(NamedTuple):
    lq: int
    n_units: int
    win_stride: int
    n_seg: int
    seg_len: int
    has_halo: bool
    pos0: int

    @property
    def tb(self):
        return self.lq * self.n_units

    @property
    def key_rows(self):
        return (self.n_units - 1) * self.win_stride + KEY_WIN


def _fill_block_diag(dst_ref, x):
    lane = lax.broadcasted_iota(jnp.int32, x.shape, 1)
    lo = lane < HEAD_DIM
    xr = pltpu.roll(x, HEAD_DIM, 1)
    zero = jnp.zeros_like(x)
    dst_ref[0, 0] = jnp.where(lo, x, zero).astype(BF16)
    dst_ref[0, 1] = jnp.where(lo, zero, xr).astype(BF16)
    dst_ref[1, 0] = jnp.where(lo, xr, zero).astype(BF16)
    dst_ref[1, 1] = jnp.where(lo, zero, x).astype(BF16)


def _attention(cfg, q_ref, kbd_ref, vbd_ref, bias_ref, sink_ref, attn_ref, first_block):
    lq = cfg.lq

    for n in range(cfg.n_units):
        off_q = n * lq
        off_w = n * cfg.win_stride
        bias_idx = jnp.where(first_block, 1, 0) if (cfg.has_halo and n == 0) else 0
        for kv in range(N_KV_HEADS):
            cols = [LANES * (PAIRS_PER_KV * kv + p) for p in range(PAIRS_PER_KV)]
            q2 = jnp.concatenate([q_ref[pl.ds(off_q, lq), c:c + HEAD_PAIR] for c in cols], axis=0)
            kbd = jnp.concatenate([kbd_ref[kv, 0, pl.ds(off_w, KEY_WIN), :],
                                   kbd_ref[kv, 1, pl.ds(off_w, KEY_WIN), :]], axis=0)
            vbd = jnp.concatenate([vbd_ref[kv, 0, pl.ds(off_w, KEY_WIN), :],
                                   vbd_ref[kv, 1, pl.ds(off_w, KEY_WIN), :]], axis=0)
            s_all = lax.dot_general(q2, kbd, (((1,), (1,)), ((), ())), preferred_element_type=F32)
            probs, dens = [], []
            for p in range(PAIRS_PER_KV):
                pair_e = []
                for j in range(2):
                    head = kv * GROUP + 2 * p + j
                    s = s_all[p * lq:(p + 1) * lq, j * KEY_WIN:(j + 1) * KEY_WIN] + bias_ref[bias_idx, head]
                    sink = sink_ref[head] * LOG2E
                    m = jnp.maximum(jnp.max(s, axis=-1, keepdims=True), sink)
                    e = jnp.exp2(s - m)
                    dens.append(jnp.sum(e, axis=-1, keepdims=True) + jnp.exp2(sink - m))
                    pair_e.append(e.astype(BF16))
                probs.append(jnp.concatenate(pair_e, axis=1))
            o_all = _dot(jnp.concatenate(probs, axis=0), vbd)
            lane = lax.broadcasted_iota(jnp.int32, (lq, HEAD_PAIR), 1)
            for p in range(PAIRS_PER_KV):
                den = jnp.where(lane < HEAD_DIM, dens[2 * p], dens[2 * p + 1])
                o = o_all[p * lq:(p + 1) * lq] / den
                attn_ref[pl.ds(off_q, lq), cols[p]:cols[p] + HEAD_PAIR] = o.astype(BF16)


def _pooling(cfg, ext_ref, wpg_ref, scale_ref, pool_ref, pos_base):
    ln = cfg.seg_len
    row = lax.broadcasted_iota(jnp.int32, (ln, 1), 0)
    for seg in range(cfg.n_seg):
        pos = pos_base + row
        for g, w in enumerate(POOL_WINDOWS):
            lanes = slice(g * POOL_GROUP_DIM, (g + 1) * POOL_GROUP_DIM)
            u = ext_ref[seg, HIST_ROWS:HIST_ROWS + ln, lanes]
            acc = u
            for back in range(1, w):
                acc = acc + ext_ref[seg, HIST_ROWS - back:HIST_ROWS - back + ln, lanes]
            cnt = jnp.minimum(pos + 1, w).astype(F32)
            mixed = (acc / cnt - u).astype(BF16)
            y = _dot(mixed, wpg_ref[g]) * scale_ref[:, lanes]
            pool_ref[seg * ln:(seg + 1) * ln, lanes] = y.astype(BF16)


def _mixer_kernel(cfg, *refs):
    if cfg.has_halo:
        (sink_ref, q_ref, kh_ref, k_ref, vh_ref, v_ref, uh_ref, u_ref, gate_ref, h_ref, bias_ref,
         wau_ref, wpg_ref, scale_ref, wpu_ref, wout_ref, gpost_ref, out_ref,
         kbd_ref, vbd_ref, ext_ref, attn_ref, pool_ref) = refs
    else:
        (sink_ref, q_ref, k_ref, v_ref, uh_ref, u_ref, gate_ref, h_ref, bias_ref,
         wau_ref, wpg_ref, scale_ref, wpu_ref, wout_ref, gpost_ref, out_ref,
         kbd_ref, vbd_ref, ext_ref, attn_ref, pool_ref) = refs

    step = pl.program_id(1) if cfg.has_halo else pl.program_id(0)
    first_block = step == 0

    if cfg.has_halo:
        _fill_block_diag(kbd_ref, jnp.concatenate([kh_ref[...], k_ref[...]], axis=0))
        _fill_block_diag(vbd_ref, jnp.concatenate([vh_ref[...], v_ref[...]], axis=0))
    else:
        _fill_block_diag(kbd_ref, k_ref[...])
        _fill_block_diag(vbd_ref, v_ref[...])
    _attention(cfg, q_ref, kbd_ref, vbd_ref, bias_ref, sink_ref, attn_ref, first_block)

    if cfg.has_halo:
        hist = uh_ref[...]
        ext_ref[0, 0:HIST_ROWS, :] = jnp.where(first_block, jnp.zeros_like(hist), hist)
        ext_ref[0, HIST_ROWS:, :] = u_ref[...]
        pos_base = cfg.pos0 + step * cfg.tb
    else:
        for seg in range(cfg.n_seg):
            ext_ref[seg, 0:HIST_ROWS, :] = uh_ref[seg]
            ext_ref[seg, HIST_ROWS:, :] = u_ref[seg * cfg.seg_len:(seg + 1) * cfg.seg_len, :]
        pos_base = cfg.pos0
    _pooling(cfg, ext_ref, wpg_ref, scale_ref, pool_ref, pos_base)

    a = _dot(attn_ref[...], wau_ref[...])
    p = _dot(pool_ref[...], wpu_ref[...])
    merged = gate_ref[:, :D_MODEL] * a + gate_ref[:, D_MODEL:] * p
    o = _dot(merged.astype(BF16), wout_ref[...])
    out_ref[...] = h_ref[...] + _rms(o, gpost_ref[...])


def _mixer_weights_specs():
    return [_const_spec((ATTN_WIDTH, D_MODEL)),
            _const_spec((len(POOL_WINDOWS), POOL_GROUP_DIM, POOL_GROUP_DIM)),
            _const_spec((1, POOL_WIDTH)),
            _const_spec((POOL_WIDTH, D_MODEL)),
            _const_spec((D_MODEL, D_MODEL)),
            _const_spec((1, D_MODEL))]


def _mixer_scratch(cfg):
    return [pltpu.VMEM((N_KV_HEADS, 2, cfg.key_rows, KV_WIDTH), BF16),
            pltpu.VMEM((N_KV_HEADS, 2, cfg.key_rows, KV_WIDTH), BF16),
            pltpu.VMEM((cfg.n_seg, HIST_ROWS + cfg.seg_len, POOL_WIDTH), F32),
            pltpu.VMEM((cfg.tb, ATTN_WIDTH), BF16),
            pltpu.VMEM((cfg.tb, POOL_WIDTH), BF16)]


def _mixer_prompt(cfg, sinks, q, k, v, u, gates, h, bias, weights):
    b, t = q.shape[:2]
    tb = cfg.tb
    halo_k = tb // WINDOW
    halo_u = tb // HIST_ROWS
    cur = lambda width: pl.BlockSpec((None, tb, width), lambda bi, i: (bi, i, 0))
    prev_k = pl.BlockSpec((None, WINDOW, KV_WIDTH), lambda bi, i: (bi, jnp.maximum(i * halo_k - 1, 0), 0))
    prev_u = pl.BlockSpec((None, HIST_ROWS, POOL_WIDTH), lambda bi, i: (bi, jnp.maximum(i * halo_u - 1, 0), 0))
    return pl.pallas_call(
        functools.partial(_mixer_kernel, cfg),
        grid=(b, t // tb),
        in_specs=[pl.BlockSpec(memory_space=pltpu.SMEM),
                  cur(ATTN_WIDTH), prev_k, cur(KV_WIDTH), prev_k, cur(KV_WIDTH),
                  prev_u, cur(POOL_WIDTH), cur(GATE_WIDTH), cur(D_MODEL),
                  _const_spec(bias.shape)] + _mixer_weights_specs(),
        out_specs=cur(D_MODEL),
        out_shape=jax.ShapeDtypeStruct((b, t, D_MODEL), F32),
        scratch_shapes=_mixer_scratch(cfg),
        compiler_params=_params(2),
    )(sinks, q, k, k, v, v, u, u, gates, h, bias, *weights)


def _mixer_sample(cfg, sinks, q, k_win, v_win, u_hist, u, gates, h, bias, weights):
    n = q.shape[0]
    tb = cfg.tb
    row = lambda width: pl.BlockSpec((tb, width), lambda i: (i, 0))
    return pl.pallas_call(
        functools.partial(_mixer_kernel, cfg),
        grid=(n // tb,),
        in_specs=[pl.BlockSpec(memory_space=pltpu.SMEM),
                  row(ATTN_WIDTH),
                  pl.BlockSpec((cfg.key_rows, KV_WIDTH), lambda i: (i, 0)),
                  pl.BlockSpec((cfg.key_rows, KV_WIDTH), lambda i: (i, 0)),
                  pl.BlockSpec((cfg.n_seg, HIST_ROWS, POOL_WIDTH), lambda i: (i, 0, 0)),
                  row(POOL_WIDTH), row(GATE_WIDTH), row(D_MODEL),
                  _const_spec(bias.shape)] + _mixer_weights_specs(),
        out_specs=row(D_MODEL),
        out_shape=jax.ShapeDtypeStruct((n, D_MODEL), F32),
        scratch_shapes=_mixer_scratch(cfg),
        compiler_params=_params(1),
    )(sinks, q, k_win, v_win, u_hist, u, gates, h, bias, *weights)


def _alibi_bias(lq, q_off, valid_fn):
    slopes = 2.0 ** (-8.0 * jnp.arange(1, N_HEADS + 1, dtype=F32) / N_HEADS)
    r = jnp.arange(lq)[:, None]
    j = jnp.arange(KEY_WIN)[None, :]
    dist = jnp.abs(r + q_off - j).astype(F32)
    tables = []
    for valid in valid_fn(r, j):
        tables.append(jnp.where(valid[None], -(slopes[:, None, None] * dist[None]) * LOG2E, NEG_INF))
    return jnp.stack(tables)


def _swiglu_chunk(x, wg_ref, wu_ref, wd_ref):
    fc = wg_ref.shape[1]
    spans = [(s, min(s + FF_SUB, fc)) for s in range(0, fc, FF_SUB)]
    gate_up = [(_dot(x, wg_ref[:, a:b]), _dot(x, wu_ref[:, a:b])) for a, b in spans]
    y = None
    for (g, u), (a, b) in zip(gate_up, spans):
        part = _dot((jax.nn.silu(g) * u).astype(BF16), wd_ref[a:b, :])
        y = part if y is None else y + part
    return y


def _ffn_epilogue(h, f, gpost_ref, p_ref, wple_ref, wpg_ref):
    h2 = h + _rms(f, gpost_ref[...])
    emb = _dot(p_ref[...].astype(BF16), wple_ref[...])
    return h2 + emb * jax.nn.sigmoid(_dot(h2.astype(BF16), wpg_ref[...]))


def _ffn_kernel(h_ref, gpre_ref, wg_ref, wu_ref, wd_ref, gpost_ref, p_ref, wple_ref, wpg_ref,
                out_ref, hn_ref, acc_ref):
    f = pl.program_id(1)

    @pl.when(f == 0)
    def _():
        hn_ref[...] = _rms(h_ref[...], gpre_ref[...]).astype(BF16)
        acc_ref[...] = _swiglu_chunk(hn_ref[...], wg_ref, wu_ref, wd_ref)

    @pl.when(f > 0)
    def _():
        acc_ref[...] += _swiglu_chunk(hn_ref[...], wg_ref, wu_ref, wd_ref)

    @pl.when(f == pl.num_programs(1) - 1)
    def _():
        out_ref[...] = _ffn_epilogue(h_ref[...], acc_ref[...], gpost_ref, p_ref, wple_ref, wpg_ref)


def _ffn(h, gpre, wg, wu, wd, gpost, p, w_ple, w_ple_gate, tm):
    n = h.shape[0]
    fc = wg.shape[2]
    row = lambda width: pl.BlockSpec((tm, width), lambda i, f: (i, 0))
    const = lambda shape: pl.BlockSpec(shape, lambda i, f: (0,) * len(shape))
    return pl.pallas_call(
        _ffn_kernel,
        grid=(n // tm, wg.shape[0]),
        in_specs=[row(D_MODEL), const((1, D_MODEL)),
                  pl.BlockSpec((None, D_MODEL, fc), lambda i, f: (f, 0, 0)),
                  pl.BlockSpec((None, D_MODEL, fc), lambda i, f: (f, 0, 0)),
                  pl.BlockSpec((fc, D_MODEL), lambda i, f: (f, 0)),
                  const((1, D_MODEL)), row(PLE_DIM), const((PLE_DIM, D_MODEL)), const((D_MODEL, D_MODEL))],
        out_specs=row(D_MODEL),
        out_shape=jax.ShapeDtypeStruct((n, D_MODEL), F32),
        scratch_shapes=[pltpu.VMEM((tm, D_MODEL), BF16), pltpu.VMEM((tm, D_MODEL), F32)],
        compiler_params=_params(2),
    )(h, gpre, wg, wu, wd, gpost, p, w_ple, w_ple_gate)


SUBLANES = 8
ROUTE_TILE = 512
EXPERT_ROWS = 1024
EXPERT_ROWS_SMALL = 256
ZERO_ROWS = 128
REC_E1, REC_E2, REC_POS1, REC_POS2, REC_W1, REC_W2 = range(6)


def _sorted_rows(tm):
    return 2 * tm + LANES


def _route_kernel(h_ref, gpre_ref, wr_ref, hn_ref, rec_ref, cnt_ref):
    tm = h_ref.shape[0]
    hn = _rms(h_ref[...], gpre_ref[...]).astype(BF16)
    hn_ref[...] = hn
    logits = _dot(hn, wr_ref[...])
    lane = lax.broadcasted_iota(jnp.int32, logits.shape, 1).astype(F32)
    logits = jnp.where(lane < N_EXPERTS, logits, -jnp.inf)
    m1 = jnp.max(logits, axis=-1, keepdims=True)
    i1 = jnp.min(jnp.where(logits == m1, lane, float(LANES)), axis=-1, keepdims=True)
    rest = jnp.where(lane == i1, -jnp.inf, logits)
    m2 = jnp.max(rest, axis=-1, keepdims=True)
    i2 = jnp.min(jnp.where(rest == m2, lane, float(LANES)), axis=-1, keepdims=True)
    t = jnp.exp(m2 - m1)
    den = 1.0 + t

    sel = jnp.where(jnp.logical_or(lane == i1, lane == i2), 1.0, 0.0)
    r = lax.broadcasted_iota(jnp.int32, (tm, tm), 0)
    c = lax.broadcasted_iota(jnp.int32, (tm, tm), 1)
    rank = _dot(jnp.where(c < r, 1.0, 0.0).astype(BF16), sel.astype(BF16))
    cnt = jnp.sum(sel, axis=0, keepdims=True)
    padded = jnp.ceil(cnt * (1.0 / SUBLANES)) * SUBLANES
    er = lax.broadcasted_iota(jnp.int32, (LANES, LANES), 0)
    ec = lax.broadcasted_iota(jnp.int32, (LANES, LANES), 1)
    seg_start = _dot(jnp.broadcast_to(padded, (SUBLANES, LANES)).astype(BF16),
                     jnp.where(er < ec, 1.0, 0.0).astype(BF16))[0:1]
    pos = seg_start + rank
    pos1 = jnp.sum(jnp.where(lane == i1, pos, 0.0), axis=-1, keepdims=True)
    pos2 = jnp.sum(jnp.where(lane == i2, pos, 0.0), axis=-1, keepdims=True)
    rec = jnp.zeros_like(logits)
    for idx, val in ((REC_E1, i1), (REC_E2, i2), (REC_POS1, pos1), (REC_POS2, pos2),
                     (REC_W1, 1.0 / den), (REC_W2, t / den)):
        rec = jnp.where(lane == idx, val, rec)
    rec_ref[...] = rec
    cnt_ref[...] = cnt


def _route(h, gpre, w_router, tm):
    n = h.shape[0]
    row = lambda width: pl.BlockSpec((tm, width), lambda i: (i, 0))
    return pl.pallas_call(
        _route_kernel,
        grid=(n // tm,),
        in_specs=[row(D_MODEL), _const_spec((1, D_MODEL)), _const_spec((D_MODEL, LANES))],
        out_specs=[row(D_MODEL), row(LANES), pl.BlockSpec((None, 1, LANES), lambda i: (i, 0, 0))],
        out_shape=[jax.ShapeDtypeStruct((n, D_MODEL), BF16),
                   jax.ShapeDtypeStruct((n, LANES), F32),
                   jax.ShapeDtypeStruct((n // tm, 1, LANES), F32)],
        compiler_params=_params(1),
    )(h, gpre, w_router)


def _group_copy(src_ref, dst_ref, src_row, dst_row, sem):
    return pltpu.make_async_copy(src_ref.at[pl.ds(src_row, SUBLANES)], dst_ref.at[pl.ds(dst_row, SUBLANES)], sem)


def _for_each_group(tile, pc_ref, lb_ref, base_ref, fn):
    for e in range(N_EXPERTS):
        idx = tile * N_EXPERTS + e
        local0 = lb_ref[idx]
        global0 = base_ref[idx]

        def body(g, carry):
            fn(pl.multiple_of(local0 + g * SUBLANES, SUBLANES), pl.multiple_of(global0 + g * SUBLANES, SUBLANES))
            return carry

        lax.fori_loop(0, pc_ref[idx] // SUBLANES, body, 0)


def _tile_groups(tile, pc_ref, lb_ref):
    last = tile * N_EXPERTS + N_EXPERTS - 1
    return (lb_ref[last] + pc_ref[last]) // SUBLANES


def _wait_groups(n_groups, src_ref, dst_ref, sem):
    @pl.when(n_groups > 0)
    def _():
        rows = n_groups * SUBLANES
        pltpu.make_async_copy(src_ref.at[pl.ds(0, rows)], dst_ref.at[pl.ds(0, rows)], sem).wait()


def _zero_copy(zero_ref, xs_ref, row, n_rows, sem):
    return pltpu.make_async_copy(zero_ref.at[pl.ds(0, n_rows)], xs_ref.at[pl.ds(row, n_rows)], sem)


def _dispatch_kernel(pc_ref, lb_ref, base_ref, tail_ref, hn_ref, rec_ref, xs_ref, tile_ref, zero_ref, sem, zsem):
    tile = pl.program_id(0)
    n_tiles = pl.num_programs(0)
    slot = tile % 2
    tm = hn_ref.shape[0]
    rec_t = rec_ref[...].T
    row = lax.broadcasted_iota(jnp.int32, (_sorted_rows(tm), tm), 0).astype(F32)
    hit = jnp.logical_or(row == rec_t[REC_POS1:REC_POS1 + 1], row == rec_t[REC_POS2:REC_POS2 + 1])
    tile_ref[slot] = _dot(jnp.where(hit, 1.0, 0.0).astype(BF16), hn_ref[...])
    _for_each_group(tile, pc_ref, lb_ref, base_ref,
                    lambda lo, gl: _group_copy(tile_ref.at[slot], xs_ref, lo, gl, sem.at[slot]).start())

    @pl.when(tile > 0)
    def _():
        _wait_groups(_tile_groups(tile - 1, pc_ref, lb_ref), tile_ref.at[1 - slot], xs_ref, sem.at[1 - slot])

    @pl.when(tile == n_tiles - 1)
    def _():
        _wait_groups(_tile_groups(tile, pc_ref, lb_ref), tile_ref.at[slot], xs_ref, sem.at[slot])
        zero_ref[...] = jnp.zeros_like(zero_ref)
        for e in range(N_EXPERTS):
            start = tail_ref[e]
            lax.fori_loop(0, tail_ref[N_EXPERTS + e], lambda g, c, start=start: (_zero_copy(
                zero_ref, xs_ref, pl.multiple_of(start + g * SUBLANES, SUBLANES), SUBLANES, zsem.at[0]).start(), c)[1], 0)
            lax.fori_loop(0, tail_ref[N_EXPERTS + e], lambda g, c: (_zero_copy(
                zero_ref, xs_ref, 0, SUBLANES, zsem.at[0]).wait(), c)[1], 0)
        used_rows = tail_ref[2 * N_EXPERTS]
        n_blocks = (xs_ref.shape[0] - used_rows) // ZERO_ROWS
        lax.fori_loop(0, n_blocks, lambda g, c: (_zero_copy(
            zero_ref, xs_ref, pl.multiple_of(used_rows + g * ZERO_ROWS, ZERO_ROWS), ZERO_ROWS, zsem.at[1]).start(), c)[1], 0)
        lax.fori_loop(0, n_blocks, lambda g, c: (_zero_copy(
            zero_ref, xs_ref, 0, ZERO_ROWS, zsem.at[1]).wait(), c)[1], 0)


def _dispatch(pc, lb, base, tail, hn, rec, max_rows, tm):
    n = hn.shape[0]
    row = lambda width: pl.BlockSpec((tm, width), lambda i, *_: (i, 0))
    return pl.pallas_call(
        _dispatch_kernel,
        grid_spec=pltpu.PrefetchScalarGridSpec(
            num_scalar_prefetch=4,
            grid=(n // tm,),
            in_specs=[row(D_MODEL), row(LANES)],
            out_specs=pl.BlockSpec(memory_space=pl.ANY),
            scratch_shapes=[pltpu.VMEM((2, _sorted_rows(tm), D_MODEL), F32),
                            pltpu.VMEM((ZERO_ROWS, D_MODEL), F32),
                            pltpu.SemaphoreType.DMA((2,)), pltpu.SemaphoreType.DMA((2,))]),
        out_shape=jax.ShapeDtypeStruct((max_rows, D_MODEL), F32),
        compiler_params=_params(1),
    )(pc, lb, base, tail, hn, rec)


def _experts_kernel(eid_ref, nused_ref, x_ref, wg_ref, wu_ref, wd_ref, y_ref, xb_ref):
    del eid_ref
    f = pl.program_id(1)
    used = pl.program_id(0) < nused_ref[0]

    @pl.when(jnp.logical_and(used, f == 0))
    def _():
        xb_ref[...] = x_ref[...].astype(BF16)
        y_ref[...] = _swiglu_chunk(xb_ref[...], wg_ref, wu_ref, wd_ref)

    @pl.when(jnp.logical_and(used, f > 0))
    def _():
        y_ref[...] += _swiglu_chunk(xb_ref[...], wg_ref, wu_ref, wd_ref)

    @pl.when(jnp.logical_and(jnp.logical_not(used), f == 0))
    def _():
        y_ref[...] = jnp.zeros_like(y_ref)


def _experts(eid, n_used, xs, wg, wu, wd, rt):
    rows = xs.shape[0]
    n_f, fc = wg.shape[1], wg.shape[3]
    tile = lambda r, nu: jnp.minimum(r, nu[0] - 1)
    chunk = lambda r, f, nu: jnp.where(r < nu[0], f, n_f - 1)
    return pl.pallas_call(
        _experts_kernel,
        grid_spec=pltpu.PrefetchScalarGridSpec(
            num_scalar_prefetch=2,
            grid=(rows // rt, n_f),
            in_specs=[pl.BlockSpec((rt, D_MODEL), lambda r, f, eid, nu: (tile(r, nu), 0)),
                      pl.BlockSpec((None, None, D_MODEL, fc),
                                   lambda r, f, eid, nu: (eid[tile(r, nu)], chunk(r, f, nu), 0, 0)),
                      pl.BlockSpec((None, None, D_MODEL, fc),
                                   lambda r, f, eid, nu: (eid[tile(r, nu)], chunk(r, f, nu), 0, 0)),
                      pl.BlockSpec((None, fc, D_MODEL), lambda r, f, eid, nu: (eid[tile(r, nu)], chunk(r, f, nu), 0))],
            out_specs=pl.BlockSpec((rt, D_MODEL), lambda r, f, eid, nu: (r, 0)),
            scratch_shapes=[pltpu.VMEM((rt, D_MODEL), BF16)]),
        out_shape=jax.ShapeDtypeStruct((rows, D_MODEL), F32),
        compiler_params=_params(2),
    )(eid, n_used, xs, wg, wu, wd)


def _combine_kernel(pc_ref, lb_ref, base_ref, rec_ref, h_ref, gpost_ref, p_ref, wple_ref, wpg_ref, ys_ref,
                    out_ref, tile_ref, sem):
    tile = pl.program_id(0)
    slot = tile % 2
    tm = h_ref.shape[0]
    rows = _sorted_rows(tm)

    def fetch(t, s):
        _for_each_group(t, pc_ref, lb_ref, base_ref,
                        lambda lo, gl: _group_copy(ys_ref, tile_ref.at[s], gl, lo, sem.at[s]).start())

    @pl.when(tile == 0)
    def _():
        fetch(tile, slot)

    @pl.when(tile + 1 < pl.num_programs(0))
    def _():
        fetch(tile + 1, 1 - slot)

    rec = rec_ref[...]
    col = lax.broadcasted_iota(jnp.int32, (tm, rows), 1).astype(F32)
    weights = (jnp.where(col == rec[:, REC_POS1:REC_POS1 + 1], rec[:, REC_W1:REC_W1 + 1], 0.0)
               + jnp.where(col == rec[:, REC_POS2:REC_POS2 + 1], rec[:, REC_W2:REC_W2 + 1], 0.0))
    n_groups = _tile_groups(tile, pc_ref, lb_ref)
    _wait_groups(n_groups, ys_ref, tile_ref.at[slot], sem.at[slot])
    live = lax.broadcasted_iota(jnp.int32, (rows, 1), 0) < n_groups * SUBLANES
    y = jnp.where(live, tile_ref[slot], 0.0).astype(BF16)
    f = _dot(weights.astype(BF16), y)
    out_ref[...] = _ffn_epilogue(h_ref[...], f, gpost_ref, p_ref, wple_ref, wpg_ref)


def _combine(pc, lb, base, rec, h, gpost, p, w_ple, w_ple_gate, ys, tm):
    n = h.shape[0]
    row = lambda width: pl.BlockSpec((tm, width), lambda i, *_: (i, 0))
    const = lambda shape: pl.BlockSpec(shape, lambda i, *_: (0,) * len(shape))
    return pl.pallas_call(
        _combine_kernel,
        grid_spec=pltpu.PrefetchScalarGridSpec(
            num_scalar_prefetch=3,
            grid=(n // tm,),
            in_specs=[row(LANES), row(D_MODEL), const((1, D_MODEL)), row(PLE_DIM), const((PLE_DIM, D_MODEL)),
                      const((D_MODEL, D_MODEL)), pl.BlockSpec(memory_space=pl.ANY)],
            out_specs=row(D_MODEL),
            scratch_shapes=[pltpu.VMEM((2, _sorted_rows(tm), D_MODEL), F32), pltpu.SemaphoreType.DMA((2,))]),
        out_shape=jax.ShapeDtypeStruct((n, D_MODEL), F32),
        compiler_params=_params(1),
    )(pc, lb, base, rec, h, gpost, p, w_ple, w_ple_gate, ys)


def _moe(h, gpre, w_router, wg, wu, wd, gpost, p, w_ple, w_ple_gate):
    n = h.shape[0]
    tm = _row_tile(n, ROUTE_TILE)
    n_tiles = n // tm
    hn, rec, cnt = _route(h, gpre, w_router, tm)
    rt = EXPERT_ROWS if 2 * n >= 4 * N_EXPERTS * EXPERT_ROWS else EXPERT_ROWS_SMALL

    cnt = cnt[:, 0, :N_EXPERTS].astype(jnp.int32)
    pc = (cnt + SUBLANES - 1) // SUBLANES * SUBLANES
    lb = jnp.cumsum(pc, axis=1) - pc
    region = (jnp.sum(pc, axis=0) + rt - 1) // rt * rt
    region_end = jnp.cumsum(region)
    base = (region_end - region)[None, :] + jnp.cumsum(pc, axis=0) - pc
    max_rows = 2 * n + n_tiles * N_EXPERTS * (SUBLANES - 1) + N_EXPERTS * (rt - 1)
    max_rows = (max_rows + rt - 1) // rt * rt
    tile_start = jnp.arange(max_rows // rt, dtype=jnp.int32) * rt
    eid = jnp.minimum(jnp.sum(tile_start[:, None] >= region_end[None, :], axis=1), N_EXPERTS - 1).astype(jnp.int32)
    n_used = (region_end[-1:] // rt).astype(jnp.int32)
    total = jnp.sum(pc, axis=0)
    tail = jnp.concatenate([region_end - region + total, (region - total) // SUBLANES, region_end[-1:]])
    flat = lambda a: a.reshape(-1).astype(jnp.int32)

    xs = _dispatch(flat(pc), flat(lb), flat(base), flat(tail), hn, rec, max_rows, tm)
    ys = _experts(eid, n_used, xs, wg, wu, wd, rt)
    return _combine(flat(pc), flat(lb), flat(base), rec, h, gpost, p, w_ple, w_ple_gate, ys, tm)


def _row_tile(n, target):
    tm = min(n, target)
    assert n % tm == 0, (n, tm)
    return tm


def _column_chunks_kernel(w_ref, o_ref):
    o_ref[...] = w_ref[...].astype(BF16)


def _column_chunks(w, fc):
    e, d, ff = w.shape
    assert ff % fc == 0 and fc % LANES == 0
    return pl.pallas_call(
        _column_chunks_kernel,
        grid=(e, ff // fc),
        in_specs=[pl.BlockSpec((None, d, fc), lambda i, f: (i, 0, f))],
        out_specs=pl.BlockSpec((None, None, d, fc), lambda i, f: (i, f, 0, 0)),
        out_shape=jax.ShapeDtypeStruct((e, ff // fc, d, fc), BF16),
        compiler_params=_params(2),
    )(w)


def _prepare(prm):
    out = dict(prm)
    for name in ('w_in', 'w_attn_up', 'w_pool_group', 'w_pool_up', 'w_out', 'w_down_dense', 'w_down_moe',
                 'w_ple', 'w_ple_gate'):
        out[name] = prm[name].astype(BF16)
    out['w_router'] = jnp.pad(prm['w_router'], ((0, 0), (0, 0), (0, LANES - N_EXPERTS))).astype(BF16)
    out['w_gate_dense'] = _column_chunks(prm['w_gate_dense'], FF_CHUNK_DENSE)
    out['w_up_dense'] = _column_chunks(prm['w_up_dense'], FF_CHUNK_DENSE)
    n_moe, n_exp = prm['w_gate_moe'].shape[:2]
    for name in ('w_gate_moe', 'w_up_moe'):
        w = _column_chunks(prm[name].reshape((n_moe * n_exp,) + prm[name].shape[2:]), FF_CHUNK_MOE)
        out[name] = w.reshape((n_moe, n_exp) + w.shape[1:])
    return out


def _trunk(x, p, caches, prm, pos0):
    b, t, _ = x.shape
    n = b * t
    depth = prm['w_in'].shape[0]
    h = x.reshape(n, D_MODEL)
    vec = lambda a: a.reshape(1, -1)
    ks, vs, us = [], [], []

    if caches is None:
        cfg = MixerCfg(lq=2 * CHUNK, n_units=4, win_stride=WINDOW, n_seg=1, seg_len=8 * CHUNK,
                       has_halo=True, pos0=pos0)
        assert t % cfg.tb == 0
        bias = _alibi_bias(cfg.lq, WINDOW, lambda r, j: (
            (j // CHUNK >= r // CHUNK) & (j // CHUNK <= r // CHUNK + 2),
            (j // CHUNK >= r // CHUNK) & (j // CHUNK <= r // CHUNK + 2) & (j >= WINDOW)))
    else:
        seqs = 8
        cfg = MixerCfg(lq=t, n_units=seqs, win_stride=KEY_WIN, n_seg=seqs, seg_len=t,
                       has_halo=False, pos0=pos0)
        assert b % seqs == 0 and WINDOW + t <= KEY_WIN
        bias = _alibi_bias(cfg.lq, WINDOW, lambda r, j: ((j < WINDOW + t) & (r >= 0),))

    for i in range(depth):
        q, k, v, u, gates = _in_proj(h, vec(prm['g_mix_pre'][i]), prm['w_in'][i],
                                     vec(prm['b_in'][i]), _row_tile(n, 512))
        weights = (prm['w_attn_up'][i], prm['w_pool_group'][i], vec(prm['pool_scale'][i]),
                   prm['w_pool_up'][i], prm['w_out'][i], vec(prm['g_mix_post'][i]))
        sinks = prm['attn_sinks'][i]
        k3 = k.reshape(b, t, KV_WIDTH)
        v3 = v.reshape(b, t, KV_WIDTH)
        u3 = u.reshape(b, t, POOL_WIDTH)
        if caches is None:
            h = _mixer_prompt(cfg, sinks, q.reshape(b, t, ATTN_WIDTH), k3, v3, u3,
                              gates.reshape(b, t, GATE_WIDTH), h.reshape(b, t, D_MODEL),
                              bias, weights).reshape(n, D_MODEL)
            k_all, v_all, u_all = k3, v3, u3
        else:
            k_all = jnp.concatenate([caches[0][i].reshape(b, WINDOW, KV_WIDTH), k3], axis=1)
            v_all = jnp.concatenate([caches[1][i].reshape(b, WINDOW, KV_WIDTH), v3], axis=1)
            u_all = jnp.concatenate([caches[2][i], u3], axis=1)
            pad = ((0, 0), (0, KEY_WIN - WINDOW - t), (0, 0))
            k_win = jnp.pad(k_all, pad).reshape(b * KEY_WIN, KV_WIDTH)
            v_win = jnp.pad(v_all, pad).reshape(b * KEY_WIN, KV_WIDTH)
            u_hist = jnp.pad(caches[2][i], ((0, 0), (HIST_ROWS - POOL_HIST, 0), (0, 0)))
            h = _mixer_sample(cfg, sinks, q, k_win, v_win, u_hist, u, gates, h, bias, weights)
        ks.append(k_all[:, -WINDOW:].reshape(b, WINDOW, N_KV_HEADS, HEAD_DIM))
        vs.append(v_all[:, -WINDOW:].reshape(b, WINDOW, N_KV_HEADS, HEAD_DIM))
        us.append(u_all[:, -POOL_HIST:])

        j = i // 2
        tail = (vec(prm['g_ffn_post'][i]), p[i].reshape(n, PLE_DIM), prm['w_ple'][i], prm['w_ple_gate'][i])
        if i % 2 == 0:
            h = _ffn(h, vec(prm['g_ffn_pre'][i]), prm['w_gate_dense'][j], prm['w_up_dense'][j],
                     prm['w_down_dense'][j], *tail, _row_tile(n, 512))
        else:
            h = _moe(h, vec(prm['g_ffn_pre'][i]), prm['w_router'][j], prm['w_gate_moe'][j],
                     prm['w_up_moe'][j], prm['w_down_moe'][j], *tail)
    return h.reshape(b, t, D_MODEL), jnp.stack(ks), jnp.stack(vs), jnp.stack(us)


def kernel(x_prompt, x_sample, cache_k, cache_v, state_pool, p_prompt, p_sample, w_in, b_in, attn_sinks, w_attn_up, w_pool_group, pool_scale, w_pool_up, w_out, g_mix_pre, g_mix_post, g_ffn_pre, g_ffn_post, w_gate_dense, w_up_dense, w_down_dense, w_router, w_gate_moe, w_up_moe, w_down_moe, w_ple, w_ple_gate):
    prm = {
        'w_in': w_in, 'b_in': b_in, 'attn_sinks': attn_sinks, 'w_attn_up': w_attn_up,
        'w_pool_group': w_pool_group, 'pool_scale': pool_scale, 'w_pool_up': w_pool_up,
        'w_out': w_out, 'g_mix_pre': g_mix_pre, 'g_mix_post': g_mix_post,
        'g_ffn_pre': g_ffn_pre, 'g_ffn_post': g_ffn_post,
        'w_gate_dense': w_gate_dense, 'w_up_dense': w_up_dense, 'w_down_dense': w_down_dense,
        'w_router': w_router, 'w_gate_moe': w_gate_moe, 'w_up_moe': w_up_moe,
        'w_down_moe': w_down_moe, 'w_ple': w_ple, 'w_ple_gate': w_ple_gate,
    }
    prm = _prepare(prm)
    y_prompt, k_prompt, v_prompt, pool_prompt = _trunk(x_prompt, p_prompt, None, prm, 0)
    y_sample, k_sample, v_sample, pool_sample = _trunk(
        x_sample, p_sample, (cache_k, cache_v, state_pool), prm, PAST_LEN)
    return (y_prompt, y_sample, k_prompt, v_prompt, pool_prompt, k_sample, v_sample, pool_sample)
```

```python
import functools
from typing import NamedTuple

import jax
import jax.numpy as jnp
from jax import lax
from jax.experimental import pallas as pl
from jax.experimental.pallas import tpu as pltpu

F32 = jnp.float32
BF16 = jnp.bfloat16

D_MODEL = 1024
CHUNK = 64
WINDOW = 128
N_HEADS = 16
N_KV_HEADS = 2
GROUP = N_HEADS // N_KV_HEADS
HEAD_DIM = 64
ATTN_WIDTH = N_HEADS * HEAD_DIM
KV_WIDTH = N_KV_HEADS * HEAD_DIM
POOL_WINDOWS = (2, 4, 8, 16)
POOL_GROUP_DIM = 128
POOL_WIDTH = len(POOL_WINDOWS) * POOL_GROUP_DIM
POOL_HIST = max(POOL_WINDOWS) - 1
GATE_WIDTH = 2 * D_MODEL
IN_WIDTH = ATTN_WIDTH + 2 * KV_WIDTH + POOL_WIDTH + GATE_WIDTH
N_EXPERTS = 8
PLE_DIM = 256
PAST_LEN = 2048
EPS = 1e-6
NEG_INF = -1e30
LOG2E = 1.4426950408889634

LANES = 128
V7X_VMEM_BYTES = 64 * 1024 * 1024
VMEM_LIMIT = V7X_VMEM_BYTES * 7 // 8

HEAD_PAIR = 2 * HEAD_DIM
PAIRS_PER_KV = GROUP // 2
KEY_WIN = 2 * WINDOW
HIST_ROWS = POOL_HIST + 1
COL_CHUNK = 512
FF_CHUNK_DENSE = 2048
FF_CHUNK_MOE = 1792
FF_SUB = 256


def _rms(x, g):
    return x * lax.rsqrt(jnp.mean(x * x, axis=-1, keepdims=True) + EPS) * g


def _dot(a, b):
    return jnp.dot(a, b, preferred_element_type=F32)


def _params(n_grid):
    return pltpu.CompilerParams(dimension_semantics=("arbitrary",) * n_grid,
                                vmem_limit_bytes=VMEM_LIMIT)


def _const_spec(shape):
    return pl.BlockSpec(shape, lambda *_: (0,) * len(shape))


def _in_proj_kernel(h_ref, g_ref, w_ref, b_ref, q_ref, k_ref, v_ref, u_ref, gate_ref):
    xn = _rms(h_ref[...], g_ref[...]).astype(BF16)

    def proj(lo, hi):
        return _dot(xn, w_ref[:, lo:hi]) + b_ref[:, lo:hi]

    c1 = ATTN_WIDTH
    c2 = c1 + KV_WIDTH
    c3 = c2 + KV_WIDTH
    c4 = c3 + POOL_WIDTH
    for lo in range(0, c1, COL_CHUNK):
        q_ref[:, lo:lo + COL_CHUNK] = (proj(lo, lo + COL_CHUNK) * (HEAD_DIM ** -0.5 * LOG2E)).astype(BF16)
    kv = proj(c1, c3)
    k_ref[...] = kv[:, :KV_WIDTH]
    v_ref[...] = kv[:, KV_WIDTH:]
    u_ref[...] = proj(c3, c4)
    for lo in range(0, GATE_WIDTH, COL_CHUNK):
        gate_ref[:, lo:lo + COL_CHUNK] = jax.nn.sigmoid(proj(c4 + lo, c4 + lo + COL_CHUNK)).astype(BF16)


def _in_proj(h, g, w, b, tm):
    n = h.shape[0]
    row = lambda width: pl.BlockSpec((tm, width), lambda i: (i, 0))
    return pl.pallas_call(
        _in_proj_kernel,
        grid=(n // tm,),
        in_specs=[row(D_MODEL), _const_spec((1, D_MODEL)), _const_spec((D_MODEL, IN_WIDTH)),
                  _const_spec((1, IN_WIDTH))],
        out_specs=[row(ATTN_WIDTH), row(KV_WIDTH), row(KV_WIDTH), row(POOL_WIDTH), row(GATE_WIDTH)],
        out_shape=[jax.ShapeDtypeStruct((n, ATTN_WIDTH), BF16),
                   jax.ShapeDtypeStruct((n, KV_WIDTH), F32),
                   jax.ShapeDtypeStruct((n, KV_WIDTH), F32),
                   jax.ShapeDtypeStruct((n, POOL_WIDTH), F32),
                   jax.ShapeDtypeStruct((n, GATE_WIDTH), BF16)],
        compiler_params=_params(1),
    )(h, g, w, b)


class MixerCfg(NamedTuple):
    lq: int
    n_units: int
    win_stride: int
    n_seg: int
    seg_len: int
    has_halo: bool
    pos0: int

    @property
    def tb(self):
        return self.lq * self.n_units

    @property
    def key_rows(self):
        return (self.n_units - 1) * self.win_stride + KEY_WIN


def _fill_block_diag(dst_ref, x):
    lane = lax.broadcasted_iota(jnp.int32, x.shape, 1)
    lo = lane < HEAD_DIM
    xr = pltpu.roll(x, HEAD_DIM, 1)
    zero = jnp.zeros_like(x)
    dst_ref[0, 0] = jnp.where(lo, x, zero).astype(BF16)
    dst_ref[0, 1] = jnp.where(lo, zero, xr).astype(BF16)
    dst_ref[1, 0] = jnp.where(lo, xr, zero).astype(BF16)
    dst_ref[1, 1] = jnp.where(lo, zero, x).astype(BF16)


def _attention(cfg, q_ref, kbd_ref, vbd_ref, bias_ref, sink_ref, attn_ref, first_block):
    lq = cfg.lq

    for n in range(cfg.n_units):
        off_q = n * lq
        off_w = n * cfg.win_stride
        bias_idx = jnp.where(first_block, 1, 0) if (cfg.has_halo and n == 0) else 0
        for kv in range(N_KV_HEADS):
            cols = [LANES * (PAIRS_PER_KV * kv + p) for p in range(PAIRS_PER_KV)]
            q2 = jnp.concatenate([q_ref[pl.ds(off_q, lq), c:c + HEAD_PAIR] for c in cols], axis=0)
            kbd = jnp.concatenate([kbd_ref[kv, 0, pl.ds(off_w, KEY_WIN), :],
                                   kbd_ref[kv, 1, pl.ds(off_w, KEY_WIN), :]], axis=0)
            vbd = jnp.concatenate([vbd_ref[kv, 0, pl.ds(off_w, KEY_WIN), :],
                                   vbd_ref[kv, 1, pl.ds(off_w, KEY_WIN), :]], axis=0)
            s_all = lax.dot_general(q2, kbd, (((1,), (1,)), ((), ())), preferred_element_type=F32)
            probs, dens = [], []
            for p in range(PAIRS_PER_KV):
                pair_e = []
                for j in range(2):
                    head = kv * GROUP + 2 * p + j
                    s = s_all[p * lq:(p + 1) * lq, j * KEY_WIN:(j + 1) * KEY_WIN] + bias_ref[bias_idx, head]
                    sink = sink_ref[head] * LOG2E
                    m = jnp.maximum(jnp.max(s, axis=-1, keepdims=True), sink)
                    e = jnp.exp2(s - m)
                    dens.append(jnp.sum(e, axis=-1, keepdims=True) + jnp.exp2(sink - m))
                    pair_e.append(e.astype(BF16))
                probs.append(jnp.concatenate(pair_e, axis=1))
            o_all = _dot(jnp.concatenate(probs, axis=0), vbd)
            lane = lax.broadcasted_iota(jnp.int32, (lq, HEAD_PAIR), 1)
            for p in range(PAIRS_PER_KV):
                den = jnp.where(lane < HEAD_DIM, dens[2 * p], dens[2 * p + 1])
                o = o_all[p * lq:(p + 1) * lq] / den
                attn_ref[pl.ds(off_q, lq), cols[p]:cols[p] + HEAD_PAIR] = o.astype(BF16)


def _pooling(cfg, ext_ref, wpg_ref, scale_ref, pool_ref, pos_base):
    ln = cfg.seg_len
    row = lax.broadcasted_iota(jnp.int32, (ln, 1), 0)
    for seg in range(cfg.n_seg):
        pos = pos_base + row
        for g, w in enumerate(POOL_WINDOWS):
            lanes = slice(g * POOL_GROUP_DIM, (g + 1) * POOL_GROUP_DIM)
            u = ext_ref[seg, HIST_ROWS:HIST_ROWS + ln, lanes]
            acc = u
            for back in range(1, w):
                acc = acc + ext_ref[seg, HIST_ROWS - back:HIST_ROWS - back + ln, lanes]
            cnt = jnp.minimum(pos + 1, w).astype(F32)
            mixed = (acc / cnt - u).astype(BF16)
            y = _dot(mixed, wpg_ref[g]) * scale_ref[:, lanes]
            pool_ref[seg * ln:(seg + 1) * ln, lanes] = y.astype(BF16)


def _mixer_kernel(cfg, *refs):
    if cfg.has_halo:
        (sink_ref, q_ref, kh_ref, k_ref, vh_ref, v_ref, uh_ref, u_ref, gate_ref, h_ref, bias_ref,
         wau_ref, wpg_ref, scale_ref, wpu_ref, wout_ref, gpost_ref, out_ref,
         kbd_ref, vbd_ref, ext_ref, attn_ref, pool_ref) = refs
    else:
        (sink_ref, q_ref, k_ref, v_ref, uh_ref, u_ref, gate_ref, h_ref, bias_ref,
         wau_ref, wpg_ref, scale_ref, wpu_ref, wout_ref, gpost_ref, out_ref,
         kbd_ref, vbd_ref, ext_ref, attn_ref, pool_ref) = refs

    step = pl.program_id(1) if cfg.has_halo else pl.program_id(0)
    first_block = step == 0

    if cfg.has_halo:
        _fill_block_diag(kbd_ref, jnp.concatenate([kh_ref[...], k_ref[...]], axis=0))
        _fill_block_diag(vbd_ref, jnp.concatenate([vh_ref[...], v_ref[...]], axis=0))
    else:
        _fill_block_diag(kbd_ref, k_ref[...])
        _fill_block_diag(vbd_ref, v_ref[...])
    _attention(cfg, q_ref, kbd_ref, vbd_ref, bias_ref, sink_ref, attn_ref, first_block)

    if cfg.has_halo:
        hist = uh_ref[...]
        ext_ref[0, 0:HIST_ROWS, :] = jnp.where(first_block, jnp.zeros_like(hist), hist)
        ext_ref[0, HIST_ROWS:, :] = u_ref[...]
        pos_base = cfg.pos0 + step * cfg.tb
    else:
        for seg in range(cfg.n_seg):
            ext_ref[seg, 0:HIST_ROWS, :] = uh_ref[seg]
            ext_ref[seg, HIST_ROWS:, :] = u_ref[seg * cfg.seg_len:(seg + 1) * cfg.seg_len, :]
        pos_base = cfg.pos0
    _pooling(cfg, ext_ref, wpg_ref, scale_ref, pool_ref, pos_base)

    a = _dot(attn_ref[...], wau_ref[...])
    p = _dot(pool_ref[...], wpu_ref[...])
    merged = gate_ref[:, :D_MODEL].astype(F32) * a + gate_ref[:, D_MODEL:].astype(F32) * p
    o = _dot(merged.astype(BF16), wout_ref[...])
    out_ref[...] = h_ref[...] + _rms(o, gpost_ref[...])


def _mixer_weights_specs():
    return [_const_spec((ATTN_WIDTH, D_MODEL)),
            _const_spec((len(POOL_WINDOWS), POOL_GROUP_DIM, POOL_GROUP_DIM)),
            _const_spec((1, POOL_WIDTH)),
            _const_spec((POOL_WIDTH, D_MODEL)),
            _const_spec((D_MODEL, D_MODEL)),
            _const_spec((1, D_MODEL))]


def _mixer_scratch(cfg):
    return [pltpu.VMEM((N_KV_HEADS, 2, cfg.key_rows, KV_WIDTH), BF16),
            pltpu.VMEM((N_KV_HEADS, 2, cfg.key_rows, KV_WIDTH), BF16),
            pltpu.VMEM((cfg.n_seg, HIST_ROWS + cfg.seg_len, POOL_WIDTH), F32),
            pltpu.VMEM((cfg.tb, ATTN_WIDTH), BF16),
            pltpu.VMEM((cfg.tb, POOL_WIDTH), BF16)]


def _mixer_prompt(cfg, sinks, q, k, v, u, gates, h, bias, weights):
    b, t = q.shape[:2]
    tb = cfg.tb
    halo_k = tb // WINDOW
    halo_u = tb // HIST_ROWS
    cur = lambda width: pl.BlockSpec((None, tb, width), lambda bi, i: (bi, i, 0))
    prev_k = pl.BlockSpec((None, WINDOW, KV_WIDTH), lambda bi, i: (bi, jnp.maximum(i * halo_k - 1, 0), 0))
    prev_u = pl.BlockSpec((None, HIST_ROWS, POOL_WIDTH), lambda bi, i: (bi, jnp.maximum(i * halo_u - 1, 0), 0))
    return pl.pallas_call(
        functools.partial(_mixer_kernel, cfg),
        grid=(b, t // tb),
        in_specs=[pl.BlockSpec(memory_space=pltpu.SMEM),
                  cur(ATTN_WIDTH), prev_k, cur(KV_WIDTH), prev_k, cur(KV_WIDTH),
                  prev_u, cur(POOL_WIDTH), cur(GATE_WIDTH), cur(D_MODEL),
                  _const_spec(bias.shape)] + _mixer_weights_specs(),
        out_specs=cur(D_MODEL),
        out_shape=jax.ShapeDtypeStruct((b, t, D_MODEL), F32),
        scratch_shapes=_mixer_scratch(cfg),
        compiler_params=_params(2),
    )(sinks, q, k, k, v, v, u, u, gates, h, bias, *weights)


def _mixer_sample(cfg, sinks, q, k_win, v_win, u_hist, u, gates, h, bias, weights):
    n = q.shape[0]
    tb = cfg.tb
    row = lambda width: pl.BlockSpec((tb, width), lambda i: (i, 0))
    return pl.pallas_call(
        functools.partial(_mixer_kernel, cfg),
        grid=(n // tb,),
        in_specs=[pl.BlockSpec(memory_space=pltpu.SMEM),
                  row(ATTN_WIDTH),
                  pl.BlockSpec((cfg.key_rows, KV_WIDTH), lambda i: (i, 0)),
                  pl.BlockSpec((cfg.key_rows, KV_WIDTH), lambda i: (i, 0)),
                  pl.BlockSpec((cfg.n_seg, HIST_ROWS, POOL_WIDTH), lambda i: (i, 0, 0)),
                  row(POOL_WIDTH), row(GATE_WIDTH), row(D_MODEL),
                  _const_spec(bias.shape)] + _mixer_weights_specs(),
        out_specs=row(D_MODEL),
        out_shape=jax.ShapeDtypeStruct((n, D_MODEL), F32),
        scratch_shapes=_mixer_scratch(cfg),
        compiler_params=_params(1),
    )(sinks, q, k_win, v_win, u_hist, u, gates, h, bias, *weights)


def _alibi_bias(lq, q_off, valid_fn):
    slopes = 2.0 ** (-8.0 * jnp.arange(1, N_HEADS + 1, dtype=F32) / N_HEADS)
    r = jnp.arange(lq)[:, None]
    j = jnp.arange(KEY_WIN)[None, :]
    dist = jnp.abs(r + q_off - j).astype(F32)
    tables = []
    for valid in valid_fn(r, j):
        tables.append(jnp.where(valid[None], -(slopes[:, None, None] * dist[None]) * LOG2E, NEG_INF))
    return jnp.stack(tables)


def _swiglu_chunk(x, wg_ref, wu_ref, wd_ref):
    fc = wg_ref.shape[1]
    spans = [(s, min(s + FF_SUB, fc)) for s in range(0, fc, FF_SUB)]
    gate_up = [(_dot(x, wg_ref[:, a:b]), _dot(x, wu_ref[:, a:b])) for a, b in spans]
    y = None
    for (g, u), (a, b) in zip(gate_up, spans):
        part = _dot((jax.nn.silu(g) * u).astype(BF16), wd_ref[a:b, :])
        y = part if y is None else y + part
    return y


def _ffn_epilogue(h, f, gpost_ref, p_ref, wple_ref, wpg_ref):
    h2 = h + _rms(f, gpost_ref[...])
    emb = _dot(p_ref[...].astype(BF16), wple_ref[...])
    return h2 + emb * jax.nn.sigmoid(_dot(h2.astype(BF16), wpg_ref[...]))


def _ffn_kernel(h_ref, gpre_ref, wg_ref, wu_ref, wd_ref, gpost_ref, p_ref, wple_ref, wpg_ref,
                out_ref, hn_ref, acc_ref):
    f = pl.program_id(1)

    @pl.when(f == 0)
    def _():
        hn_ref[...] = _rms(h_ref[...], gpre_ref[...]).astype(BF16)
        acc_ref[...] = _swiglu_chunk(hn_ref[...], wg_ref, wu_ref, wd_ref)

    @pl.when(f > 0)
    def _():
        acc_ref[...] += _swiglu_chunk(hn_ref[...], wg_ref, wu_ref, wd_ref)

    @pl.when(f == pl.num_programs(1) - 1)
    def _():
        out_ref[...] = _ffn_epilogue(h_ref[...], acc_ref[...], gpost_ref, p_ref, wple_ref, wpg_ref)


def _ffn(h, gpre, wg, wu, wd, gpost, p, w_ple, w_ple_gate, tm):
    n = h.shape[0]
    fc = wg.shape[2]
    row = lambda width: pl.BlockSpec((tm, width), lambda i, f: (i, 0))
    const = lambda shape: pl.BlockSpec(shape, lambda i, f: (0,) * len(shape))
    return pl.pallas_call(
        _ffn_kernel,
        grid=(n // tm, wg.shape[0]),
        in_specs=[row(D_MODEL), const((1, D_MODEL)),
                  pl.BlockSpec((None, D_MODEL, fc), lambda i, f: (f, 0, 0)),
                  pl.BlockSpec((None, D_MODEL, fc), lambda i, f: (f, 0, 0)),
                  pl.BlockSpec((fc, D_MODEL), lambda i, f: (f, 0)),
                  const((1, D_MODEL)), pl.BlockSpec((None, tm, PLE_DIM), lambda i, f: (p[1], i, 0)),
                  const((PLE_DIM, D_MODEL)), const((D_MODEL, D_MODEL))],
        out_specs=row(D_MODEL),
        out_shape=jax.ShapeDtypeStruct((n, D_MODEL), F32),
        scratch_shapes=[pltpu.VMEM((tm, D_MODEL), BF16), pltpu.VMEM((tm, D_MODEL), F32)],
        compiler_params=_params(2),
    )(h, gpre, wg, wu, wd, gpost, p[0], w_ple, w_ple_gate)


SUBLANES = 8
COPY_ROWS = 32
ROUTE_TILE = 512
EXPERT_ROWS = 1024
EXPERT_ROWS_SMALL = 256
ZERO_ROWS = 128
REC_E1, REC_E2, REC_POS1, REC_POS2, REC_W1, REC_W2 = range(6)


def _sorted_rows(tm):
    return 2 * tm + LANES


def _route_kernel(h_ref, gpre_ref, wr_ref, hn_ref, rec_ref, cnt_ref):
    tm = h_ref.shape[0]
    hn = _rms(h_ref[...], gpre_ref[...]).astype(BF16)
    hn_ref[...] = hn
    logits = _dot(hn, wr_ref[...])
    lane = lax.broadcasted_iota(jnp.int32, logits.shape, 1).astype(F32)
    logits = jnp.where(lane < N_EXPERTS, logits, -jnp.inf)
    m1 = jnp.max(logits, axis=-1, keepdims=True)
    i1 = jnp.min(jnp.where(logits == m1, lane, float(LANES)), axis=-1, keepdims=True)
    rest = jnp.where(lane == i1, -jnp.inf, logits)
    m2 = jnp.max(rest, axis=-1, keepdims=True)
    i2 = jnp.min(jnp.where(rest == m2, lane, float(LANES)), axis=-1, keepdims=True)
    t = jnp.exp(m2 - m1)
    den = 1.0 + t

    sel = jnp.where(jnp.logical_or(lane == i1, lane == i2), 1.0, 0.0)
    r = lax.broadcasted_iota(jnp.int32, (tm, tm), 0)
    c = lax.broadcasted_iota(jnp.int32, (tm, tm), 1)
    rank = _dot(jnp.where(c < r, 1.0, 0.0).astype(BF16), sel.astype(BF16))
    cnt = jnp.sum(sel, axis=0, keepdims=True)
    padded = jnp.ceil(cnt * (1.0 / SUBLANES)) * SUBLANES
    er = lax.broadcasted_iota(jnp.int32, (LANES, LANES), 0)
    ec = lax.broadcasted_iota(jnp.int32, (LANES, LANES), 1)
    seg_start = _dot(jnp.broadcast_to(padded, (SUBLANES, LANES)).astype(BF16),
                     jnp.where(er < ec, 1.0, 0.0).astype(BF16))[0:1]
    pos = seg_start + rank
    pos1 = jnp.sum(jnp.where(lane == i1, pos, 0.0), axis=-1, keepdims=True)
    pos2 = jnp.sum(jnp.where(lane == i2, pos, 0.0), axis=-1, keepdims=True)
    rec = jnp.zeros_like(logits)
    for idx, val in ((REC_E1, i1), (REC_E2, i2), (REC_POS1, pos1), (REC_POS2, pos2),
                     (REC_W1, 1.0 / den), (REC_W2, t / den)):
        rec = jnp.where(lane == idx, val, rec)
    rec_ref[...] = rec
    cnt_ref[...] = cnt


def _route(h, gpre, w_router, tm):
    n = h.shape[0]
    row = lambda width: pl.BlockSpec((tm, width), lambda i: (i, 0))
    return pl.pallas_call(
        _route_kernel,
        grid=(n // tm,),
        in_specs=[row(D_MODEL), _const_spec((1, D_MODEL)), _const_spec((D_MODEL, LANES))],
        out_specs=[row(D_MODEL), row(LANES), pl.BlockSpec((None, 1, LANES), lambda i: (i, 0, 0))],
        out_shape=[jax.ShapeDtypeStruct((n, D_MODEL), BF16),
                   jax.ShapeDtypeStruct((n, LANES), F32),
                   jax.ShapeDtypeStruct((n // tm, 1, LANES), F32)],
        compiler_params=_params(1),
    )(h, gpre, w_router)


def _group_copy(src_ref, dst_ref, src_row, dst_row, sem, rows=SUBLANES):
    return pltpu.make_async_copy(src_ref.at[pl.ds(src_row, rows)], dst_ref.at[pl.ds(dst_row, rows)], sem)


def _for_each_group(tile, pc_ref, lb_ref, base_ref, fn):
    for e in range(N_EXPERTS):
        idx = tile * N_EXPERTS + e
        local0 = lb_ref[idx]
        global0 = base_ref[idx]
        n_big = pc_ref[idx] // COPY_ROWS
        done = n_big * COPY_ROWS

        def big(g, carry, local0=local0, global0=global0):
            fn(pl.multiple_of(local0 + g * COPY_ROWS, SUBLANES), pl.multiple_of(global0 + g * COPY_ROWS, SUBLANES),
               COPY_ROWS)
            return carry

        def small(g, carry, local0=local0 + done, global0=global0 + done):
            fn(pl.multiple_of(local0 + g * SUBLANES, SUBLANES), pl.multiple_of(global0 + g * SUBLANES, SUBLANES),
               SUBLANES)
            return carry

        lax.fori_loop(0, n_big, big, 0)
        lax.fori_loop(0, (pc_ref[idx] - done) // SUBLANES, small, 0)


def _tile_groups(tile, pc_ref, lb_ref):
    last = tile * N_EXPERTS + N_EXPERTS - 1
    return (lb_ref[last] + pc_ref[last]) // SUBLANES


def _wait_groups(n_groups, src_ref, dst_ref, sem):
    @pl.when(n_groups > 0)
    def _():
        rows = n_groups * SUBLANES
        pltpu.make_async_copy(src_ref.at[pl.ds(0, rows)], dst_ref.at[pl.ds(0, rows)], sem).wait()


def _zero_copy(zero_ref, xs_ref, row, n_rows, sem):
    return pltpu.make_async_copy(zero_ref.at[pl.ds(0, n_rows)], xs_ref.at[pl.ds(row, n_rows)], sem)


def _dispatch_kernel(pc_ref, lb_ref, base_ref, tail_ref, hn_ref, rec_ref, xs_ref, tile_ref, zero_ref, sem, zsem):
    tile = pl.program_id(0)
    n_tiles = pl.num_programs(0)
    slot = tile % 2
    tm = hn_ref.shape[0]
    rec_t = rec_ref[...].T
    row = lax.broadcasted_iota(jnp.int32, (_sorted_rows(tm), tm), 0).astype(F32)
    hit = jnp.logical_or(row == rec_t[REC_POS1:REC_POS1 + 1], row == rec_t[REC_POS2:REC_POS2 + 1])
    tile_ref[slot] = _dot(jnp.where(hit, 1.0, 0.0).astype(BF16), hn_ref[...])
    _for_each_group(tile, pc_ref, lb_ref, base_ref,
                    lambda lo, gl, rows: _group_copy(tile_ref.at[slot], xs_ref, lo, gl, sem.at[slot], rows).start())

    @pl.when(tile > 0)
    def _():
        _wait_groups(_tile_groups(tile - 1, pc_ref, lb_ref), tile_ref.at[1 - slot], xs_ref, sem.at[1 - slot])

    @pl.when(tile == n_tiles - 1)
    def _():
        _wait_groups(_tile_groups(tile, pc_ref, lb_ref), tile_ref.at[slot], xs_ref, sem.at[slot])
        zero_ref[...] = jnp.zeros_like(zero_ref)
        for e in range(N_EXPERTS):
            start = tail_ref[e]
            lax.fori_loop(0, tail_ref[N_EXPERTS + e], lambda g, c, start=start: (_zero_copy(
                zero_ref, xs_ref, pl.multiple_of(start + g * SUBLANES, SUBLANES), SUBLANES, zsem.at[0]).start(), c)[1], 0)
            lax.fori_loop(0, tail_ref[N_EXPERTS + e], lambda g, c: (_zero_copy(
                zero_ref, xs_ref, 0, SUBLANES, zsem.at[0]).wait(), c)[1], 0)
        used_rows = tail_ref[2 * N_EXPERTS]
        n_blocks = (xs_ref.shape[0] - used_rows) // ZERO_ROWS
        lax.fori_loop(0, n_blocks, lambda g, c: (_zero_copy(
            zero_ref, xs_ref, pl.multiple_of(used_rows + g * ZERO_ROWS, ZERO_ROWS), ZERO_ROWS, zsem.at[1]).start(), c)[1], 0)
        lax.fori_loop(0, n_blocks, lambda g, c: (_zero_copy(
            zero_ref, xs_ref, 0, ZERO_ROWS, zsem.at[1]).wait(), c)[1], 0)


def _dispatch(pc, lb, base, tail, hn, rec, max_rows, tm):
    n = hn.shape[0]
    row = lambda width: pl.BlockSpec((tm, width), lambda i, *_: (i, 0))
    return pl.pallas_call(
        _dispatch_kernel,
        grid_spec=pltpu.PrefetchScalarGridSpec(
            num_scalar_prefetch=4,
            grid=(n // tm,),
            in_specs=[row(D_MODEL), row(LANES)],
            out_specs=pl.BlockSpec(memory_space=pl.ANY),
            scratch_shapes=[pltpu.VMEM((2, _sorted_rows(tm), D_MODEL), F32),
                            pltpu.VMEM((ZERO_ROWS, D_MODEL), F32),
                            pltpu.SemaphoreType.DMA((2,)), pltpu.SemaphoreType.DMA((2,))]),
        out_shape=jax.ShapeDtypeStruct((max_rows, D_MODEL), F32),
        compiler_params=_params(1),
    )(pc, lb, base, tail, hn, rec)


def _experts_kernel(eid_ref, nused_ref, x_ref, wg_ref, wu_ref, wd_ref, y_ref, xb_ref):
    del eid_ref
    f = pl.program_id(1)
    used = pl.program_id(0) < nused_ref[0]

    @pl.when(jnp.logical_and(used, f == 0))
    def _():
        xb_ref[...] = x_ref[...].astype(BF16)
        y_ref[...] = _swiglu_chunk(xb_ref[...], wg_ref, wu_ref, wd_ref)

    @pl.when(jnp.logical_and(used, f > 0))
    def _():
        y_ref[...] += _swiglu_chunk(xb_ref[...], wg_ref, wu_ref, wd_ref)

    @pl.when(jnp.logical_and(jnp.logical_not(used), f == 0))
    def _():
        y_ref[...] = jnp.zeros_like(y_ref)


def _experts(eid, n_used, xs, wg, wu, wd, rt):
    rows = xs.shape[0]
    n_f, fc = wg.shape[1], wg.shape[3]
    tile = lambda r, nu: jnp.minimum(r, nu[0] - 1)
    chunk = lambda r, f, nu: jnp.where(r < nu[0], f, n_f - 1)
    return pl.pallas_call(
        _experts_kernel,
        grid_spec=pltpu.PrefetchScalarGridSpec(
            num_scalar_prefetch=2,
            grid=(rows // rt, n_f),
            in_specs=[pl.BlockSpec((rt, D_MODEL), lambda r, f, eid, nu: (tile(r, nu), 0)),
                      pl.BlockSpec((None, None, D_MODEL, fc),
                                   lambda r, f, eid, nu: (eid[tile(r, nu)], chunk(r, f, nu), 0, 0)),
                      pl.BlockSpec((None, None, D_MODEL, fc),
                                   lambda r, f, eid, nu: (eid[tile(r, nu)], chunk(r, f, nu), 0, 0)),
                      pl.BlockSpec((None, fc, D_MODEL), lambda r, f, eid, nu: (eid[tile(r, nu)], chunk(r, f, nu), 0))],
            out_specs=pl.BlockSpec((rt, D_MODEL), lambda r, f, eid, nu: (r, 0)),
            scratch_shapes=[pltpu.VMEM((rt, D_MODEL), BF16)]),
        out_shape=jax.ShapeDtypeStruct((rows, D_MODEL), F32),
        compiler_params=_params(2),
    )(eid, n_used, xs, wg, wu, wd)


def _combine_kernel(pc_ref, lb_ref, base_ref, rec_ref, h_ref, gpost_ref, p_ref, wple_ref, wpg_ref, ys_ref,
                    out_ref, tile_ref, sem):
    tile = pl.program_id(0)
    slot = tile % 2
    tm = h_ref.shape[0]
    rows = _sorted_rows(tm)

    def fetch(t, s):
        _for_each_group(t, pc_ref, lb_ref, base_ref,
                        lambda lo, gl, rows: _group_copy(ys_ref, tile_ref.at[s], gl, lo, sem.at[s], rows).start())

    @pl.when(tile == 0)
    def _():
        fetch(tile, slot)

    @pl.when(tile + 1 < pl.num_programs(0))
    def _():
        fetch(tile + 1, 1 - slot)

    rec = rec_ref[...]
    col = lax.broadcasted_iota(jnp.int32, (tm, rows), 1).astype(F32)
    weights = (jnp.where(col == rec[:, REC_POS1:REC_POS1 + 1], rec[:, REC_W1:REC_W1 + 1], 0.0)
               + jnp.where(col == rec[:, REC_POS2:REC_POS2 + 1], rec[:, REC_W2:REC_W2 + 1], 0.0))
    n_groups = _tile_groups(tile, pc_ref, lb_ref)
    _wait_groups(n_groups, ys_ref, tile_ref.at[slot], sem.at[slot])
    live = lax.broadcasted_iota(jnp.int32, (rows, 1), 0) < n_groups * SUBLANES
    y = jnp.where(live, tile_ref[slot], 0.0).astype(BF16)
    f = _dot(weights.astype(BF16), y)
    out_ref[...] = _ffn_epilogue(h_ref[...], f, gpost_ref, p_ref, wple_ref, wpg_ref)


def _combine(pc, lb, base, rec, h, gpost, p, w_ple, w_ple_gate, ys, tm):
    n = h.shape[0]
    row = lambda width: pl.BlockSpec((tm, width), lambda i, *_: (i, 0))
    const = lambda shape: pl.BlockSpec(shape, lambda i, *_: (0,) * len(shape))
    return pl.pallas_call(
        _combine_kernel,
        grid_spec=pltpu.PrefetchScalarGridSpec(
            num_scalar_prefetch=3,
            grid=(n // tm,),
            in_specs=[row(LANES), row(D_MODEL), const((1, D_MODEL)),
                      pl.BlockSpec((None, tm, PLE_DIM), lambda i, *_: (p[1], i, 0)), const((PLE_DIM, D_MODEL)),
                      const((D_MODEL, D_MODEL)), pl.BlockSpec(memory_space=pl.ANY)],
            out_specs=row(D_MODEL),
            scratch_shapes=[pltpu.VMEM((2, _sorted_rows(tm), D_MODEL), F32), pltpu.SemaphoreType.DMA((2,))]),
        out_shape=jax.ShapeDtypeStruct((n, D_MODEL), F32),
        compiler_params=_params(1),
    )(pc, lb, base, rec, h, gpost, p[0], w_ple, w_ple_gate, ys)


def _moe(h, gpre, w_router, wg, wu, wd, gpost, p, w_ple, w_ple_gate):
    n = h.shape[0]
    tm = _row_tile(n, ROUTE_TILE)
    n_tiles = n // tm
    hn, rec, cnt = _route(h, gpre, w_router, tm)
    rt = EXPERT_ROWS if 2 * n >= 4 * N_EXPERTS * EXPERT_ROWS else EXPERT_ROWS_SMALL

    cnt = cnt[:, 0, :N_EXPERTS].astype(jnp.int32)
    pc = (cnt + SUBLANES - 1) // SUBLANES * SUBLANES
    lb = jnp.cumsum(pc, axis=1) - pc
    region = (jnp.sum(pc, axis=0) + rt - 1) // rt * rt
    region_end = jnp.cumsum(region)
    base = (region_end - region)[None, :] + jnp.cumsum(pc, axis=0) - pc
    max_rows = 2 * n + n_tiles * N_EXPERTS * (SUBLANES - 1) + N_EXPERTS * (rt - 1)
    max_rows = (max_rows + rt - 1) // rt * rt
    tile_start = jnp.arange(max_rows // rt, dtype=jnp.int32) * rt
    eid = jnp.minimum(jnp.sum(tile_start[:, None] >= region_end[None, :], axis=1), N_EXPERTS - 1).astype(jnp.int32)
    n_used = (region_end[-1:] // rt).astype(jnp.int32)
    total = jnp.sum(pc, axis=0)
    tail = jnp.concatenate([region_end - region + total, (region - total) // SUBLANES, region_end[-1:]])
    flat = lambda a: a.reshape(-1).astype(jnp.int32)

    xs = _dispatch(flat(pc), flat(lb), flat(base), flat(tail), hn, rec, max_rows, tm)
    ys = _experts(eid, n_used, xs, wg, wu, wd, rt)
    return _combine(flat(pc), flat(lb), flat(base), rec, h, gpost, p, w_ple, w_ple_gate, ys, tm)


def _row_tile(n, target):
    tm = min(n, target)
    assert n % tm == 0, (n, tm)
    return tm


def _column_chunks_kernel(w_ref, o_ref):
    o_ref[...] = w_ref[...].astype(BF16)


def _column_chunks(w, fc):
    e, d, ff = w.shape
    assert ff % fc == 0 and fc % LANES == 0
    return pl.pallas_call(
        _column_chunks_kernel,
        grid=(e, ff // fc),
        in_specs=[pl.BlockSpec((None, d, fc), lambda i, f: (i, 0, f))],
        out_specs=pl.BlockSpec((None, None, d, fc), lambda i, f: (i, f, 0, 0)),
        out_shape=jax.ShapeDtypeStruct((e, ff // fc, d, fc), BF16),
        compiler_params=_params(2),
    )(w)


def _prepare(prm):
    out = dict(prm)
    for name in ('w_in', 'w_attn_up', 'w_pool_group', 'w_pool_up', 'w_out', 'w_down_dense', 'w_down_moe',
                 'w_ple', 'w_ple_gate'):
        out[name] = prm[name].astype(BF16)
    out['w_router'] = jnp.pad(prm['w_router'], ((0, 0), (0, 0), (0, LANES - N_EXPERTS))).astype(BF16)
    out['w_gate_dense'] = _column_chunks(prm['w_gate_dense'], FF_CHUNK_DENSE)
    out['w_up_dense'] = _column_chunks(prm['w_up_dense'], FF_CHUNK_DENSE)
    n_moe, n_exp = prm['w_gate_moe'].shape[:2]
    for name in ('w_gate_moe', 'w_up_moe'):
        w = _column_chunks(prm[name].reshape((n_moe * n_exp,) + prm[name].shape[2:]), FF_CHUNK_MOE)
        out[name] = w.reshape((n_moe, n_exp) + w.shape[1:])
    return out


def _trunk(x, p, caches, prm, pos0):
    b, t, _ = x.shape
    n = b * t
    depth = prm['w_in'].shape[0]
    h = x.reshape(n, D_MODEL)
    vec = lambda a: a.reshape(1, -1)
    ks, vs, us = [], [], []

    if caches is None:
        cfg = MixerCfg(lq=2 * CHUNK, n_units=4, win_stride=WINDOW, n_seg=1, seg_len=8 * CHUNK,
                       has_halo=True, pos0=pos0)
        assert t % cfg.tb == 0
        bias = _alibi_bias(cfg.lq, WINDOW, lambda r, j: (
            (j // CHUNK >= r // CHUNK) & (j // CHUNK <= r // CHUNK + 2),
            (j // CHUNK >= r // CHUNK) & (j // CHUNK <= r // CHUNK + 2) & (j >= WINDOW)))
    else:
        seqs = 8
        cfg = MixerCfg(lq=t, n_units=seqs, win_stride=KEY_WIN, n_seg=seqs, seg_len=t,
                       has_halo=False, pos0=pos0)
        assert b % seqs == 0 and WINDOW + t <= KEY_WIN
        bias = _alibi_bias(cfg.lq, WINDOW, lambda r, j: ((j < WINDOW + t) & (r >= 0),))

    for i in range(depth):
        q, k, v, u, gates = _in_proj(h, vec(prm['g_mix_pre'][i]), prm['w_in'][i],
                                     vec(prm['b_in'][i]), _row_tile(n, 512))
        weights = (prm['w_attn_up'][i], prm['w_pool_group'][i], vec(prm['pool_scale'][i]),
                   prm['w_pool_up'][i], prm['w_out'][i], vec(prm['g_mix_post'][i]))
        sinks = prm['attn_sinks'][i]
        k3 = k.reshape(b, t, KV_WIDTH)
        v3 = v.reshape(b, t, KV_WIDTH)
        u3 = u.reshape(b, t, POOL_WIDTH)
        if caches is None:
            h = _mixer_prompt(cfg, sinks, q.reshape(b, t, ATTN_WIDTH), k3, v3, u3,
                              gates.reshape(b, t, GATE_WIDTH), h.reshape(b, t, D_MODEL),
                              bias, weights).reshape(n, D_MODEL)
            k_all, v_all, u_all = k3, v3, u3
        else:
            k_all = jnp.concatenate([caches[0][i].reshape(b, WINDOW, KV_WIDTH), k3], axis=1)
            v_all = jnp.concatenate([caches[1][i].reshape(b, WINDOW, KV_WIDTH), v3], axis=1)
            u_all = jnp.concatenate([caches[2][i], u3], axis=1)
            pad = ((0, 0), (0, KEY_WIN - WINDOW - t), (0, 0))
            k_win = jnp.pad(k_all, pad).reshape(b * KEY_WIN, KV_WIDTH)
            v_win = jnp.pad(v_all, pad).reshape(b * KEY_WIN, KV_WIDTH)
            u_hist = jnp.pad(caches[2][i], ((0, 0), (HIST_ROWS - POOL_HIST, 0), (0, 0)))
            h = _mixer_sample(cfg, sinks, q, k_win, v_win, u_hist, u, gates, h, bias, weights)
        ks.append(k_all[:, -WINDOW:].reshape(b, WINDOW, N_KV_HEADS, HEAD_DIM))
        vs.append(v_all[:, -WINDOW:].reshape(b, WINDOW, N_KV_HEADS, HEAD_DIM))
        us.append(u_all[:, -POOL_HIST:])

        j = i // 2
        tail = (vec(prm['g_ffn_post'][i]), (p.reshape(depth, n, PLE_DIM), i), prm['w_ple'][i], prm['w_ple_gate'][i])
        if i % 2 == 0:
            h = _ffn(h, vec(prm['g_ffn_pre'][i]), prm['w_gate_dense'][j], prm['w_up_dense'][j],
                     prm['w_down_dense'][j], *tail, _row_tile(n, 512))
        else:
            h = _moe(h, vec(prm['g_ffn_pre'][i]), prm['w_router'][j], prm['w_gate_moe'][j],
                     prm['w_up_moe'][j], prm['w_down_moe'][j], *tail)
    return h.reshape(b, t, D_MODEL), jnp.stack(ks), jnp.stack(vs), jnp.stack(us)


def kernel(x_prompt, x_sample, cache_k, cache_v, state_pool, p_prompt, p_sample, w_in, b_in, attn_sinks, w_attn_up, w_pool_group, pool_scale, w_pool_up, w_out, g_mix_pre, g_mix_post, g_ffn_pre, g_ffn_post, w_gate_dense, w_up_dense, w_down_dense, w_router, w_gate_moe, w_up_moe, w_down_moe, w_ple, w_ple_gate):
    prm = {
        'w_in': w_in, 'b_in': b_in, 'attn_sinks': attn_sinks, 'w_attn_up': w_attn_up,
        'w_pool_group': w_pool_group, 'pool_scale': pool_scale, 'w_pool_up': w_pool_up,
        'w_out': w_out, 'g_mix_pre': g_mix_pre, 'g_mix_post': g_mix_post,
        'g_ffn_pre': g_ffn_pre, 'g_ffn_post': g_ffn_post,
        'w_gate_dense': w_gate_dense, 'w_up_dense': w_up_dense, 'w_down_dense': w_down_dense,
        'w_router': w_router, 'w_gate_moe': w_gate_moe, 'w_up_moe': w_up_moe,
        'w_down_moe': w_down_moe, 'w_ple': w_ple, 'w_ple_gate': w_ple_gate,
    }
    prm = _prepare(prm)
    y_prompt, k_prompt, v_prompt, pool_prompt = _trunk(x_prompt, p_prompt, None, prm, 0)
    y_sample, k_sample, v_sample, pool_sample = _trunk(
        x_sample, p_sample, (cache_k, cache_v, state_pool), prm, PAST_LEN)
    return (y_prompt, y_sample, k_prompt, v_prompt, pool_prompt, k_sample, v_sample, pool_sample)
```

```python
import functools
from typing import NamedTuple

import jax
import jax.numpy as jnp
from jax import lax
from jax.experimental import pallas as pl
from jax.experimental.pallas import tpu as pltpu

F32 = jnp.float32
BF16 = jnp.bfloat16

D_MODEL = 1024
CHUNK = 64
WINDOW = 128
N_HEADS = 16
N_KV_HEADS = 2
GROUP = N_HEADS // N_KV_HEADS
HEAD_DIM = 64
ATTN_WIDTH = N_HEADS * HEAD_DIM
KV_WIDTH = N_KV_HEADS * HEAD_DIM
POOL_WINDOWS = (2, 4, 8, 16)
POOL_GROUP_DIM = 128
POOL_WIDTH = len(POOL_WINDOWS) * POOL_GROUP_DIM
POOL_HIST = max(POOL_WINDOWS) - 1
GATE_WIDTH = 2 * D_MODEL
IN_WIDTH = ATTN_WIDTH + 2 * KV_WIDTH + POOL_WIDTH + GATE_WIDTH
N_EXPERTS = 8
PLE_DIM = 256
PAST_LEN = 2048
EPS = 1e-6
NEG_INF = -1e30
LOG2E = 1.4426950408889634

LANES = 128
V7X_VMEM_BYTES = 64 * 1024 * 1024
VMEM_LIMIT = V7X_VMEM_BYTES * 7 // 8

HEAD_PAIR = 2 * HEAD_DIM
PAIRS_PER_KV = GROUP // 2
KEY_WIN = 2 * WINDOW
HIST_ROWS = POOL_HIST + 1
COL_CHUNK = 512
FF_CHUNK_DENSE = 2048
FF_CHUNK_MOE = 1792
FF_SUB = 256


def _rms(x, g):
    return x * lax.rsqrt(jnp.mean(x * x, axis=-1, keepdims=True) + EPS) * g


def _dot(a, b):
    return jnp.dot(a, b, preferred_element_type=F32)


def _params(n_grid):
    return pltpu.CompilerParams(dimension_semantics=("arbitrary",) * n_grid,
                                vmem_limit_bytes=VMEM_LIMIT)


def _const_spec(shape):
    return pl.BlockSpec(shape, lambda *_: (0,) * len(shape))


def _in_proj_kernel(h_ref, g_ref, w_ref, b_ref, q_ref, k_ref, v_ref, u_ref, gate_ref, xn_ref):
    xn_ref[...] = _rms(h_ref[...], g_ref[...]).astype(BF16)

    def proj(lo, hi):
        return _dot(xn_ref[...], w_ref[:, lo:hi]) + b_ref[:, lo:hi]

    c1 = ATTN_WIDTH
    c2 = c1 + KV_WIDTH
    c3 = c2 + KV_WIDTH
    c4 = c3 + POOL_WIDTH
    for lo in range(0, c1, COL_CHUNK):
        q_ref[:, lo:lo + COL_CHUNK] = (proj(lo, lo + COL_CHUNK) * (HEAD_DIM ** -0.5 * LOG2E)).astype(BF16)
    kv = proj(c1, c3)
    k_ref[...] = kv[:, :KV_WIDTH]
    v_ref[...] = kv[:, KV_WIDTH:]
    u_ref[...] = proj(c3, c4)
    for lo in range(0, GATE_WIDTH, COL_CHUNK):
        gate_ref[:, lo:lo + COL_CHUNK] = jax.nn.sigmoid(proj(c4 + lo, c4 + lo + COL_CHUNK)).astype(BF16)


def _in_proj(h, g, w, b, tm):
    n = h.shape[0]
    row = lambda width: pl.BlockSpec((tm, width), lambda i: (i, 0))
    return pl.pallas_call(
        _in_proj_kernel,
        grid=(n // tm,),
        in_specs=[row(D_MODEL), _const_spec((1, D_MODEL)), _const_spec((D_MODEL, IN_WIDTH)),
                  _const_spec((1, IN_WIDTH))],
        out_specs=[row(ATTN_WIDTH), row(KV_WIDTH), row(KV_WIDTH), row(POOL_WIDTH), row(GATE_WIDTH)],
        out_shape=[jax.ShapeDtypeStruct((n, ATTN_WIDTH), BF16),
                   jax.ShapeDtypeStruct((n, KV_WIDTH), F32),
                   jax.ShapeDtypeStruct((n, KV_WIDTH), F32),
                   jax.ShapeDtypeStruct((n, POOL_WIDTH), F32),
                   jax.ShapeDtypeStruct((n, GATE_WIDTH), BF16)],
        scratch_shapes=[pltpu.VMEM((tm, D_MODEL), BF16)],
        compiler_params=_params(1),
    )(h, g, w, b)


class MixerCfg(NamedTuple):
    lq: int
    n_units: int
    win_stride: int
    n_seg: int
    seg_len: int
    has_halo: bool
    pos0: int

    @property
    def tb(self):
        return self.lq * self.n_units

    @property
    def key_rows(self):
        return (self.n_units - 1) * self.win_stride + KEY_WIN


def _fill_block_diag(dst_ref, x):
    lane = lax.broadcasted_iota(jnp.int32, x.shape, 1)
    lo = lane < HEAD_DIM
    xr = pltpu.roll(x, HEAD_DIM, 1)
    zero = jnp.zeros_like(x)
    dst_ref[0, 0] = jnp.where(lo, x, zero).astype(BF16)
    dst_ref[0, 1] = jnp.where(lo, zero, xr).astype(BF16)
    dst_ref[1, 0] = jnp.where(lo, xr, zero).astype(BF16)
    dst_ref[1, 1] = jnp.where(lo, zero, x).astype(BF16)


def _attention(cfg, q_ref, kbd_ref, vbd_ref, bias_ref, sink_ref, attn_ref, first_block):
    lq = cfg.lq

    for n in range(cfg.n_units):
        off_q = n * lq
        off_w = n * cfg.win_stride
        bias_idx = jnp.where(first_block, 1, 0) if (cfg.has_halo and n == 0) else 0
        for kv in range(N_KV_HEADS):
            cols = [LANES * (PAIRS_PER_KV * kv + p) for p in range(PAIRS_PER_KV)]
            q2 = jnp.concatenate([q_ref[pl.ds(off_q, lq), c:c + HEAD_PAIR] for c in cols], axis=0)
            kbd = jnp.concatenate([kbd_ref[kv, 0, pl.ds(off_w, KEY_WIN), :],
                                   kbd_ref[kv, 1, pl.ds(off_w, KEY_WIN), :]], axis=0)
            vbd = jnp.concatenate([vbd_ref[kv, 0, pl.ds(off_w, KEY_WIN), :],
                                   vbd_ref[kv, 1, pl.ds(off_w, KEY_WIN), :]], axis=0)
            s_all = lax.dot_general(q2, kbd, (((1,), (1,)), ((), ())), preferred_element_type=F32)
            probs, dens = [], []
            for p in range(PAIRS_PER_KV):
                pair_e = []
                for j in range(2):
                    head = kv * GROUP + 2 * p + j
                    s = s_all[p * lq:(p + 1) * lq, j * KEY_WIN:(j + 1) * KEY_WIN] + bias_ref[bias_idx, head]
                    sink = sink_ref[head] * LOG2E
                    m = jnp.maximum(jnp.max(s, axis=-1, keepdims=True), sink)
                    e = jnp.exp2(s - m)
                    dens.append(jnp.sum(e, axis=-1, keepdims=True) + jnp.exp2(sink - m))
                    pair_e.append(e.astype(BF16))
                probs.append(jnp.concatenate(pair_e, axis=1))
            o_all = _dot(jnp.concatenate(probs, axis=0), vbd)
            lane = lax.broadcasted_iota(jnp.int32, (lq, HEAD_PAIR), 1)
            for p in range(PAIRS_PER_KV):
                den = jnp.where(lane < HEAD_DIM, dens[2 * p], dens[2 * p + 1])
                o = o_all[p * lq:(p + 1) * lq] / den
                attn_ref[pl.ds(off_q, lq), cols[p]:cols[p] + HEAD_PAIR] = o.astype(BF16)


def _pooling(cfg, ext_ref, wpg_ref, scale_ref, pool_ref, pos_base):
    ln = cfg.seg_len
    row = lax.broadcasted_iota(jnp.int32, (ln, 1), 0)
    for seg in range(cfg.n_seg):
        pos = pos_base + row
        for g, w in enumerate(POOL_WINDOWS):
            lanes = slice(g * POOL_GROUP_DIM, (g + 1) * POOL_GROUP_DIM)
            u = ext_ref[seg, HIST_ROWS:HIST_ROWS + ln, lanes]
            acc = u
            for back in range(1, w):
                acc = acc + ext_ref[seg, HIST_ROWS - back:HIST_ROWS - back + ln, lanes]
            cnt = jnp.minimum(pos + 1, w).astype(F32)
            mixed = (acc / cnt - u).astype(BF16)
            y = _dot(mixed, wpg_ref[g]) * scale_ref[:, lanes]
            pool_ref[seg * ln:(seg + 1) * ln, lanes] = y.astype(BF16)


def _mixer_kernel(cfg, *refs):
    if cfg.has_halo:
        (sink_ref, q_ref, kh_ref, k_ref, vh_ref, v_ref, uh_ref, u_ref, gate_ref, h_ref, bias_ref,
         wau_ref, wpg_ref, scale_ref, wpu_ref, wout_ref, gpost_ref, out_ref,
         kbd_ref, vbd_ref, ext_ref, attn_ref, pool_ref) = refs
    else:
        (sink_ref, q_ref, k_ref, v_ref, uh_ref, u_ref, gate_ref, h_ref, bias_ref,
         wau_ref, wpg_ref, scale_ref, wpu_ref, wout_ref, gpost_ref, out_ref,
         kbd_ref, vbd_ref, ext_ref, attn_ref, pool_ref) = refs

    step = pl.program_id(1) if cfg.has_halo else pl.program_id(0)
    first_block = step == 0

    if cfg.has_halo:
        _fill_block_diag(kbd_ref, jnp.concatenate([kh_ref[...], k_ref[...]], axis=0))
        _fill_block_diag(vbd_ref, jnp.concatenate([vh_ref[...], v_ref[...]], axis=0))
    else:
        _fill_block_diag(kbd_ref, k_ref[...])
        _fill_block_diag(vbd_ref, v_ref[...])
    _attention(cfg, q_ref, kbd_ref, vbd_ref, bias_ref, sink_ref, attn_ref, first_block)

    if cfg.has_halo:
        hist = uh_ref[...]
        ext_ref[0, 0:HIST_ROWS, :] = jnp.where(first_block, jnp.zeros_like(hist), hist)
        ext_ref[0, HIST_ROWS:, :] = u_ref[...]
        pos_base = cfg.pos0 + step * cfg.tb
    else:
        for seg in range(cfg.n_seg):
            ext_ref[seg, 0:HIST_ROWS, :] = uh_ref[seg]
            ext_ref[seg, HIST_ROWS:, :] = u_ref[seg * cfg.seg_len:(seg + 1) * cfg.seg_len, :]
        pos_base = cfg.pos0
    _pooling(cfg, ext_ref, wpg_ref, scale_ref, pool_ref, pos_base)

    a = _dot(attn_ref[...], wau_ref[...])
    p = _dot(pool_ref[...], wpu_ref[...])
    merged = gate_ref[:, :D_MODEL].astype(F32) * a + gate_ref[:, D_MODEL:].astype(F32) * p
    o = _dot(merged.astype(BF16), wout_ref[...])
    out_ref[...] = h_ref[...] + _rms(o, gpost_ref[...])


def _mixer_weights_specs():
    return [_const_spec((ATTN_WIDTH, D_MODEL)),
            _const_spec((len(POOL_WINDOWS), POOL_GROUP_DIM, POOL_GROUP_DIM)),
            _const_spec((1, POOL_WIDTH)),
            _const_spec((POOL_WIDTH, D_MODEL)),
            _const_spec((D_MODEL, D_MODEL)),
            _const_spec((1, D_MODEL))]


def _mixer_scratch(cfg):
    return [pltpu.VMEM((N_KV_HEADS, 2, cfg.key_rows, KV_WIDTH), BF16),
            pltpu.VMEM((N_KV_HEADS, 2, cfg.key_rows, KV_WIDTH), BF16),
            pltpu.VMEM((cfg.n_seg, HIST_ROWS + cfg.seg_len, POOL_WIDTH), F32),
            pltpu.VMEM((cfg.tb, ATTN_WIDTH), BF16),
            pltpu.VMEM((cfg.tb, POOL_WIDTH), BF16)]


def _mixer_prompt(cfg, sinks, q, k, v, u, gates, h, bias, weights):
    b, t = q.shape[:2]
    tb = cfg.tb
    halo_k = tb // WINDOW
    halo_u = tb // HIST_ROWS
    cur = lambda width: pl.BlockSpec((None, tb, width), lambda bi, i: (bi, i, 0))
    prev_k = pl.BlockSpec((None, WINDOW, KV_WIDTH), lambda bi, i: (bi, jnp.maximum(i * halo_k - 1, 0), 0))
    prev_u = pl.BlockSpec((None, HIST_ROWS, POOL_WIDTH), lambda bi, i: (bi, jnp.maximum(i * halo_u - 1, 0), 0))
    return pl.pallas_call(
        functools.partial(_mixer_kernel, cfg),
        grid=(b, t // tb),
        in_specs=[pl.BlockSpec(memory_space=pltpu.SMEM),
                  cur(ATTN_WIDTH), prev_k, cur(KV_WIDTH), prev_k, cur(KV_WIDTH),
                  prev_u, cur(POOL_WIDTH), cur(GATE_WIDTH), cur(D_MODEL),
                  _const_spec(bias.shape)] + _mixer_weights_specs(),
        out_specs=cur(D_MODEL),
        out_shape=jax.ShapeDtypeStruct((b, t, D_MODEL), F32),
        scratch_shapes=_mixer_scratch(cfg),
        compiler_params=_params(2),
    )(sinks, q, k, k, v, v, u, u, gates, h, bias, *weights)


def _mixer_sample(cfg, sinks, q, k_win, v_win, u_hist, u, gates, h, bias, weights):
    n = q.shape[0]
    tb = cfg.tb
    row = lambda width: pl.BlockSpec((tb, width), lambda i: (i, 0))
    return pl.pallas_call(
        functools.partial(_mixer_kernel, cfg),
        grid=(n // tb,),
        in_specs=[pl.BlockSpec(memory_space=pltpu.SMEM),
                  row(ATTN_WIDTH),
                  pl.BlockSpec((cfg.key_rows, KV_WIDTH), lambda i: (i, 0)),
                  pl.BlockSpec((cfg.key_rows, KV_WIDTH), lambda i: (i, 0)),
                  pl.BlockSpec((cfg.n_seg, HIST_ROWS, POOL_WIDTH), lambda i: (i, 0, 0)),
                  row(POOL_WIDTH), row(GATE_WIDTH), row(D_MODEL),
                  _const_spec(bias.shape)] + _mixer_weights_specs(),
        out_specs=row(D_MODEL),
        out_shape=jax.ShapeDtypeStruct((n, D_MODEL), F32),
        scratch_shapes=_mixer_scratch(cfg),
        compiler_params=_params(1),
    )(sinks, q, k_win, v_win, u_hist, u, gates, h, bias, *weights)


def _alibi_bias(lq, q_off, valid_fn):
    slopes = 2.0 ** (-8.0 * jnp.arange(1, N_HEADS + 1, dtype=F32) / N_HEADS)
    r = jnp.arange(lq)[:, None]
    j = jnp.arange(KEY_WIN)[None, :]
    dist = jnp.abs(r + q_off - j).astype(F32)
    tables = []
    for valid in valid_fn(r, j):
        tables.append(jnp.where(valid[None], -(slopes[:, None, None] * dist[None]) * LOG2E, NEG_INF))
    return jnp.stack(tables)


def _swiglu_chunk(x, wg_ref, wu_ref, wd_ref):
    fc = wg_ref.shape[1]
    spans = [(s, min(s + FF_SUB, fc)) for s in range(0, fc, FF_SUB)]
    gate_up = [(_dot(x, wg_ref[:, a:b]), _dot(x, wu_ref[:, a:b])) for a, b in spans]
    y = None
    for (g, u), (a, b) in zip(gate_up, spans):
        part = _dot((jax.nn.silu(g) * u).astype(BF16), wd_ref[a:b, :])
        y = part if y is None else y + part
    return y


def _ffn_epilogue(h, f, gpost_ref, emb, wpg_ref):
    h2 = h + _rms(f, gpost_ref[...])
    return h2 + emb * jax.nn.sigmoid(_dot(h2.astype(BF16), wpg_ref[...]))


def _ffn_kernel(h_ref, gpre_ref, wg_ref, wu_ref, wd_ref, gpost_ref, p_ref, wple_ref, wpg_ref,
                out_ref, hn_ref, acc_ref):
    f = pl.program_id(1)

    @pl.when(f == 0)
    def _():
        hn_ref[...] = _rms(h_ref[...], gpre_ref[...]).astype(BF16)
        acc_ref[...] = _swiglu_chunk(hn_ref[...], wg_ref, wu_ref, wd_ref)

    @pl.when(f > 0)
    def _():
        acc_ref[...] += _swiglu_chunk(hn_ref[...], wg_ref, wu_ref, wd_ref)

    @pl.when(f == pl.num_programs(1) - 1)
    def _():
        emb = _dot(p_ref[...].astype(BF16), wple_ref[...])
        out_ref[...] = _ffn_epilogue(h_ref[...], acc_ref[...], gpost_ref, emb, wpg_ref)


def _ffn(h, gpre, wg, wu, wd, gpost, p, w_ple, w_ple_gate, tm):
    n = h.shape[0]
    fc = wg.shape[2]
    row = lambda width: pl.BlockSpec((tm, width), lambda i, f: (i, 0))
    const = lambda shape: pl.BlockSpec(shape, lambda i, f: (0,) * len(shape))
    return pl.pallas_call(
        _ffn_kernel,
        grid=(n // tm, wg.shape[0]),
        in_specs=[row(D_MODEL), const((1, D_MODEL)),
                  pl.BlockSpec((None, D_MODEL, fc), lambda i, f: (f, 0, 0)),
                  pl.BlockSpec((None, D_MODEL, fc), lambda i, f: (f, 0, 0)),
                  pl.BlockSpec((fc, D_MODEL), lambda i, f: (f, 0)),
                  const((1, D_MODEL)), pl.BlockSpec((None, tm, PLE_DIM), lambda i, f: (p[1], i, 0)),
                  const((PLE_DIM, D_MODEL)), const((D_MODEL, D_MODEL))],
        out_specs=row(D_MODEL),
        out_shape=jax.ShapeDtypeStruct((n, D_MODEL), F32),
        scratch_shapes=[pltpu.VMEM((tm, D_MODEL), BF16), pltpu.VMEM((tm, D_MODEL), F32)],
        compiler_params=_params(2),
    )(h, gpre, wg, wu, wd, gpost, p[0], w_ple, w_ple_gate)


SUBLANES = 8
COPY_ROWS = 32
ROUTE_TILE = 512
EXPERT_ROWS = 1024
EXPERT_ROWS_SMALL = 256
ZERO_ROWS = 128
REC_E1, REC_E2, REC_POS1, REC_POS2, REC_W1, REC_W2 = range(6)


def _sorted_rows(tm):
    return 2 * tm + LANES


def _route_kernel(h_ref, gpre_ref, wr_ref, hn_ref, rec_ref, cnt_ref):
    tm = h_ref.shape[0]
    hn = _rms(h_ref[...], gpre_ref[...]).astype(BF16)
    hn_ref[...] = hn
    logits = _dot(hn, wr_ref[...])
    lane = lax.broadcasted_iota(jnp.int32, logits.shape, 1).astype(F32)
    logits = jnp.where(lane < N_EXPERTS, logits, -jnp.inf)
    m1 = jnp.max(logits, axis=-1, keepdims=True)
    i1 = jnp.min(jnp.where(logits == m1, lane, float(LANES)), axis=-1, keepdims=True)
    rest = jnp.where(lane == i1, -jnp.inf, logits)
    m2 = jnp.max(rest, axis=-1, keepdims=True)
    i2 = jnp.min(jnp.where(rest == m2, lane, float(LANES)), axis=-1, keepdims=True)
    t = jnp.exp(m2 - m1)
    den = 1.0 + t

    sel = jnp.where(jnp.logical_or(lane == i1, lane == i2), 1.0, 0.0)
    r = lax.broadcasted_iota(jnp.int32, (tm, tm), 0)
    c = lax.broadcasted_iota(jnp.int32, (tm, tm), 1)
    rank = _dot(jnp.where(c < r, 1.0, 0.0).astype(BF16), sel.astype(BF16))
    cnt = jnp.sum(sel, axis=0, keepdims=True)
    padded = jnp.ceil(cnt * (1.0 / SUBLANES)) * SUBLANES
    er = lax.broadcasted_iota(jnp.int32, (LANES, LANES), 0)
    ec = lax.broadcasted_iota(jnp.int32, (LANES, LANES), 1)
    seg_start = _dot(jnp.broadcast_to(padded, (SUBLANES, LANES)).astype(BF16),
                     jnp.where(er < ec, 1.0, 0.0).astype(BF16))[0:1]
    pos = seg_start + rank
    pos1 = jnp.sum(jnp.where(lane == i1, pos, 0.0), axis=-1, keepdims=True)
    pos2 = jnp.sum(jnp.where(lane == i2, pos, 0.0), axis=-1, keepdims=True)
    rec = jnp.zeros_like(logits)
    for idx, val in ((REC_E1, i1), (REC_E2, i2), (REC_POS1, pos1), (REC_POS2, pos2),
                     (REC_W1, 1.0 / den), (REC_W2, t / den)):
        rec = jnp.where(lane == idx, val, rec)
    rec_ref[...] = rec
    cnt_ref[...] = cnt


def _route(h, gpre, w_router, tm):
    n = h.shape[0]
    row = lambda width: pl.BlockSpec((tm, width), lambda i: (i, 0))
    return pl.pallas_call(
        _route_kernel,
        grid=(n // tm,),
        in_specs=[row(D_MODEL), _const_spec((1, D_MODEL)), _const_spec((D_MODEL, LANES))],
        out_specs=[row(D_MODEL), row(LANES), pl.BlockSpec((None, 1, LANES), lambda i: (i, 0, 0))],
        out_shape=[jax.ShapeDtypeStruct((n, D_MODEL), BF16),
                   jax.ShapeDtypeStruct((n, LANES), F32),
                   jax.ShapeDtypeStruct((n // tm, 1, LANES), F32)],
        compiler_params=_params(1),
    )(h, gpre, w_router)


def _group_copy(src_ref, dst_ref, src_row, dst_row, sem, rows=SUBLANES):
    return pltpu.make_async_copy(src_ref.at[pl.ds(src_row, rows)], dst_ref.at[pl.ds(dst_row, rows)], sem)


def _for_each_group(tile, pc_ref, lb_ref, base_ref, fn):
    for e in range(N_EXPERTS):
        idx = tile * N_EXPERTS + e
        local0 = lb_ref[idx]
        global0 = base_ref[idx]
        n_big = pc_ref[idx] // COPY_ROWS
        done = n_big * COPY_ROWS

        def big(g, carry, local0=local0, global0=global0):
            fn(pl.multiple_of(local0 + g * COPY_ROWS, SUBLANES), pl.multiple_of(global0 + g * COPY_ROWS, SUBLANES),
               COPY_ROWS)
            return carry

        def small(g, carry, local0=local0 + done, global0=global0 + done):
            fn(pl.multiple_of(local0 + g * SUBLANES, SUBLANES), pl.multiple_of(global0 + g * SUBLANES, SUBLANES),
               SUBLANES)
            return carry

        lax.fori_loop(0, n_big, big, 0)
        lax.fori_loop(0, (pc_ref[idx] - done) // SUBLANES, small, 0)


def _tile_groups(tile, pc_ref, lb_ref):
    last = tile * N_EXPERTS + N_EXPERTS - 1
    return (lb_ref[last] + pc_ref[last]) // SUBLANES


def _wait_groups(n_groups, src_ref, dst_ref, sem):
    @pl.when(n_groups > 0)
    def _():
        rows = n_groups * SUBLANES
        pltpu.make_async_copy(src_ref.at[pl.ds(0, rows)], dst_ref.at[pl.ds(0, rows)], sem).wait()


def _zero_copy(zero_ref, xs_ref, row, n_rows, sem):
    return pltpu.make_async_copy(zero_ref.at[pl.ds(0, n_rows)], xs_ref.at[pl.ds(row, n_rows)], sem)


def _dispatch_kernel(pc_ref, lb_ref, base_ref, tail_ref, hn_ref, rec_ref, xs_ref, tile_ref, zero_ref, sem, zsem):
    tile = pl.program_id(0)
    n_tiles = pl.num_programs(0)
    slot = tile % 2
    tm = hn_ref.shape[0]
    rec_t = rec_ref[...].T
    row = lax.broadcasted_iota(jnp.int32, (_sorted_rows(tm), tm), 0).astype(F32)
    hit = jnp.logical_or(row == rec_t[REC_POS1:REC_POS1 + 1], row == rec_t[REC_POS2:REC_POS2 + 1])
    tile_ref[slot] = _dot(jnp.where(hit, 1.0, 0.0).astype(BF16), hn_ref[...])
    _for_each_group(tile, pc_ref, lb_ref, base_ref,
                    lambda lo, gl, rows: _group_copy(tile_ref.at[slot], xs_ref, lo, gl, sem.at[slot], rows).start())

    @pl.when(tile > 0)
    def _():
        _wait_groups(_tile_groups(tile - 1, pc_ref, lb_ref), tile_ref.at[1 - slot], xs_ref, sem.at[1 - slot])

    @pl.when(tile == n_tiles - 1)
    def _():
        _wait_groups(_tile_groups(tile, pc_ref, lb_ref), tile_ref.at[slot], xs_ref, sem.at[slot])
        zero_ref[...] = jnp.zeros_like(zero_ref)
        for e in range(N_EXPERTS):
            start = tail_ref[e]
            lax.fori_loop(0, tail_ref[N_EXPERTS + e], lambda g, c, start=start: (_zero_copy(
                zero_ref, xs_ref, pl.multiple_of(start + g * SUBLANES, SUBLANES), SUBLANES, zsem.at[0]).start(), c)[1], 0)
            lax.fori_loop(0, tail_ref[N_EXPERTS + e], lambda g, c: (_zero_copy(
                zero_ref, xs_ref, 0, SUBLANES, zsem.at[0]).wait(), c)[1], 0)
        used_rows = tail_ref[2 * N_EXPERTS]
        n_blocks = (xs_ref.shape[0] - used_rows) // ZERO_ROWS
        lax.fori_loop(0, n_blocks, lambda g, c: (_zero_copy(
            zero_ref, xs_ref, pl.multiple_of(used_rows + g * ZERO_ROWS, ZERO_ROWS), ZERO_ROWS, zsem.at[1]).start(), c)[1], 0)
        lax.fori_loop(0, n_blocks, lambda g, c: (_zero_copy(
            zero_ref, xs_ref, 0, ZERO_ROWS, zsem.at[1]).wait(), c)[1], 0)


def _dispatch(pc, lb, base, tail, hn, rec, max_rows, tm):
    n = hn.shape[0]
    row = lambda width: pl.BlockSpec((tm, width), lambda i, *_: (i, 0))
    return pl.pallas_call(
        _dispatch_kernel,
        grid_spec=pltpu.PrefetchScalarGridSpec(
            num_scalar_prefetch=4,
            grid=(n // tm,),
            in_specs=[row(D_MODEL), row(LANES)],
            out_specs=pl.BlockSpec(memory_space=pl.ANY),
            scratch_shapes=[pltpu.VMEM((2, _sorted_rows(tm), D_MODEL), F32),
                            pltpu.VMEM((ZERO_ROWS, D_MODEL), F32),
                            pltpu.SemaphoreType.DMA((2,)), pltpu.SemaphoreType.DMA((2,))]),
        out_shape=jax.ShapeDtypeStruct((max_rows, D_MODEL), F32),
        compiler_params=_params(1),
    )(pc, lb, base, tail, hn, rec)


def _experts_kernel(eid_ref, nused_ref, x_ref, wg_ref, wu_ref, wd_ref, y_ref, xb_ref):
    del eid_ref
    f = pl.program_id(1)
    used = pl.program_id(0) < nused_ref[0]

    @pl.when(jnp.logical_and(used, f == 0))
    def _():
        xb_ref[...] = x_ref[...].astype(BF16)
        y_ref[...] = _swiglu_chunk(xb_ref[...], wg_ref, wu_ref, wd_ref)

    @pl.when(jnp.logical_and(used, f > 0))
    def _():
        y_ref[...] += _swiglu_chunk(xb_ref[...], wg_ref, wu_ref, wd_ref)

    @pl.when(jnp.logical_and(jnp.logical_not(used), f == 0))
    def _():
        y_ref[...] = jnp.zeros_like(y_ref)


def _experts(eid, n_used, xs, wg, wu, wd, rt):
    rows = xs.shape[0]
    n_f, fc = wg.shape[1], wg.shape[3]
    tile = lambda r, nu: jnp.minimum(r, nu[0] - 1)
    chunk = lambda r, f, nu: jnp.where(r < nu[0], f, n_f - 1)
    return pl.pallas_call(
        _experts_kernel,
        grid_spec=pltpu.PrefetchScalarGridSpec(
            num_scalar_prefetch=2,
            grid=(rows // rt, n_f),
            in_specs=[pl.BlockSpec((rt, D_MODEL), lambda r, f, eid, nu: (tile(r, nu), 0)),
                      pl.BlockSpec((None, None, D_MODEL, fc),
                                   lambda r, f, eid, nu: (eid[tile(r, nu)], chunk(r, f, nu), 0, 0)),
                      pl.BlockSpec((None, None, D_MODEL, fc),
                                   lambda r, f, eid, nu: (eid[tile(r, nu)], chunk(r, f, nu), 0, 0)),
                      pl.BlockSpec((None, fc, D_MODEL), lambda r, f, eid, nu: (eid[tile(r, nu)], chunk(r, f, nu), 0))],
            out_specs=pl.BlockSpec((rt, D_MODEL), lambda r, f, eid, nu: (r, 0)),
            scratch_shapes=[pltpu.VMEM((rt, D_MODEL), BF16)]),
        out_shape=jax.ShapeDtypeStruct((rows, D_MODEL), F32),
        compiler_params=_params(2),
    )(eid, n_used, xs, wg, wu, wd)


def _combine_kernel(pc_ref, lb_ref, base_ref, rec_ref, h_ref, gpost_ref, p_ref, wple_ref, wpg_ref, ys_ref,
                    out_ref, tile_ref, sem):
    tile = pl.program_id(0)
    slot = tile % 2
    tm = h_ref.shape[0]
    rows = _sorted_rows(tm)

    def fetch(t, s):
        _for_each_group(t, pc_ref, lb_ref, base_ref,
                        lambda lo, gl, rows: _group_copy(ys_ref, tile_ref.at[s], gl, lo, sem.at[s], rows).start())

    @pl.when(tile == 0)
    def _():
        fetch(tile, slot)

    @pl.when(tile + 1 < pl.num_programs(0))
    def _():
        fetch(tile + 1, 1 - slot)

    rec = rec_ref[...]
    col = lax.broadcasted_iota(jnp.int32, (tm, rows), 1).astype(F32)
    weights = (jnp.where(col == rec[:, REC_POS1:REC_POS1 + 1], rec[:, REC_W1:REC_W1 + 1], 0.0)
               + jnp.where(col == rec[:, REC_POS2:REC_POS2 + 1], rec[:, REC_W2:REC_W2 + 1], 0.0))
    emb = _dot(p_ref[...].astype(BF16), wple_ref[...])
    n_groups = _tile_groups(tile, pc_ref, lb_ref)
    _wait_groups(n_groups, ys_ref, tile_ref.at[slot], sem.at[slot])
    live = lax.broadcasted_iota(jnp.int32, (rows, 1), 0) < n_groups * SUBLANES
    y = jnp.where(live, tile_ref[slot], 0.0).astype(BF16)
    f = _dot(weights.astype(BF16), y)
    out_ref[...] = _ffn_epilogue(h_ref[...], f, gpost_ref, emb, wpg_ref)


def _combine(pc, lb, base, rec, h, gpost, p, w_ple, w_ple_gate, ys, tm):
    n = h.shape[0]
    row = lambda width: pl.BlockSpec((tm, width), lambda i, *_: (i, 0))
    const = lambda shape: pl.BlockSpec(shape, lambda i, *_: (0,) * len(shape))
    return pl.pallas_call(
        _combine_kernel,
        grid_spec=pltpu.PrefetchScalarGridSpec(
            num_scalar_prefetch=3,
            grid=(n // tm,),
            in_specs=[row(LANES), row(D_MODEL), const((1, D_MODEL)),
                      pl.BlockSpec((None, tm, PLE_DIM), lambda i, *_: (p[1], i, 0)), const((PLE_DIM, D_MODEL)),
                      const((D_MODEL, D_MODEL)), pl.BlockSpec(memory_space=pl.ANY)],
            out_specs=row(D_MODEL),
            scratch_shapes=[pltpu.VMEM((2, _sorted_rows(tm), D_MODEL), F32), pltpu.SemaphoreType.DMA((2,))]),
        out_shape=jax.ShapeDtypeStruct((n, D_MODEL), F32),
        compiler_params=_params(1),
    )(pc, lb, base, rec, h, gpost, p[0], w_ple, w_ple_gate, ys)


def _moe(h, gpre, w_router, wg, wu, wd, gpost, p, w_ple, w_ple_gate):
    n = h.shape[0]
    tm = _row_tile(n, ROUTE_TILE)
    n_tiles = n // tm
    hn, rec, cnt = _route(h, gpre, w_router, tm)
    rt = EXPERT_ROWS if 2 * n >= 4 * N_EXPERTS * EXPERT_ROWS else EXPERT_ROWS_SMALL

    cnt = cnt[:, 0, :N_EXPERTS].astype(jnp.int32)
    pc = (cnt + SUBLANES - 1) // SUBLANES * SUBLANES
    lb = jnp.cumsum(pc, axis=1) - pc
    region = (jnp.sum(pc, axis=0) + rt - 1) // rt * rt
    region_end = jnp.cumsum(region)
    base = (region_end - region)[None, :] + jnp.cumsum(pc, axis=0) - pc
    max_rows = 2 * n + n_tiles * N_EXPERTS * (SUBLANES - 1) + N_EXPERTS * (rt - 1)
    max_rows = (max_rows + rt - 1) // rt * rt
    tile_start = jnp.arange(max_rows // rt, dtype=jnp.int32) * rt
    eid = jnp.minimum(jnp.sum(tile_start[:, None] >= region_end[None, :], axis=1), N_EXPERTS - 1).astype(jnp.int32)
    n_used = (region_end[-1:] // rt).astype(jnp.int32)
    total = jnp.sum(pc, axis=0)
    tail = jnp.concatenate([region_end - region + total, (region - total) // SUBLANES, region_end[-1:]])
    flat = lambda a: a.reshape(-1).astype(jnp.int32)

    xs = _dispatch(flat(pc), flat(lb), flat(base), flat(tail), hn, rec, max_rows, tm)
    ys = _experts(eid, n_used, xs, wg, wu, wd, rt)
    return _combine(flat(pc), flat(lb), flat(base), rec, h, gpost, p, w_ple, w_ple_gate, ys, tm)


def _row_tile(n, target):
    tm = min(n, target)
    assert n % tm == 0, (n, tm)
    return tm


def _column_chunks_kernel(w_ref, o_ref):
    o_ref[...] = w_ref[...].astype(BF16)


def _column_chunks(w, fc):
    e, d, ff = w.shape
    assert ff % fc == 0 and fc % LANES == 0
    return pl.pallas_call(
        _column_chunks_kernel,
        grid=(e, ff // fc),
        in_specs=[pl.BlockSpec((None, d, fc), lambda i, f: (i, 0, f))],
        out_specs=pl.BlockSpec((None, None, d, fc), lambda i, f: (i, f, 0, 0)),
        out_shape=jax.ShapeDtypeStruct((e, ff // fc, d, fc), BF16),
        compiler_params=_params(2),
    )(w)


def _prepare(prm):
    out = dict(prm)
    for name in ('w_in', 'w_attn_up', 'w_pool_group', 'w_pool_up', 'w_out', 'w_down_dense', 'w_down_moe',
                 'w_ple', 'w_ple_gate'):
        out[name] = prm[name].astype(BF16)
    out['w_router'] = jnp.pad(prm['w_router'], ((0, 0), (0, 0), (0, LANES - N_EXPERTS))).astype(BF16)
    out['w_gate_dense'] = _column_chunks(prm['w_gate_dense'], FF_CHUNK_DENSE)
    out['w_up_dense'] = _column_chunks(prm['w_up_dense'], FF_CHUNK_DENSE)
    n_moe, n_exp = prm['w_gate_moe'].shape[:2]
    for name in ('w_gate_moe', 'w_up_moe'):
        w = _column_chunks(prm[name].reshape((n_moe * n_exp,) + prm[name].shape[2:]), FF_CHUNK_MOE)
        out[name] = w.reshape((n_moe, n_exp) + w.shape[1:])
    return out


def _trunk(x, p, caches, prm, pos0):
    b, t, _ = x.shape
    n = b * t
    depth = prm['w_in'].shape[0]
    h = x.reshape(n, D_MODEL)
    vec = lambda a: a.reshape(1, -1)
    ks, vs, us = [], [], []

    if caches is None:
        cfg = MixerCfg(lq=2 * CHUNK, n_units=4, win_stride=WINDOW, n_seg=1, seg_len=8 * CHUNK,
                       has_halo=True, pos0=pos0)
        assert t % cfg.tb == 0
        bias = _alibi_bias(cfg.lq, WINDOW, lambda r, j: (
            (j // CHUNK >= r // CHUNK) & (j // CHUNK <= r // CHUNK + 2),
            (j // CHUNK >= r // CHUNK) & (j // CHUNK <= r // CHUNK + 2) & (j >= WINDOW)))
    else:
        seqs = 8
        cfg = MixerCfg(lq=t, n_units=seqs, win_stride=KEY_WIN, n_seg=seqs, seg_len=t,
                       has_halo=False, pos0=pos0)
        assert b % seqs == 0 and WINDOW + t <= KEY_WIN
        bias = _alibi_bias(cfg.lq, WINDOW, lambda r, j: ((j < WINDOW + t) & (r >= 0),))

    for i in range(depth):
        q, k, v, u, gates = _in_proj(h, vec(prm['g_mix_pre'][i]), prm['w_in'][i],
                                     vec(prm['b_in'][i]), _row_tile(n, 1024))
        weights = (prm['w_attn_up'][i], prm['w_pool_group'][i], vec(prm['pool_scale'][i]),
                   prm['w_pool_up'][i], prm['w_out'][i], vec(prm['g_mix_post'][i]))
        sinks = prm['attn_sinks'][i]
        k3 = k.reshape(b, t, KV_WIDTH)
        v3 = v.reshape(b, t, KV_WIDTH)
        u3 = u.reshape(b, t, POOL_WIDTH)
        if caches is None:
            h = _mixer_prompt(cfg, sinks, q.reshape(b, t, ATTN_WIDTH), k3, v3, u3,
                              gates.reshape(b, t, GATE_WIDTH), h.reshape(b, t, D_MODEL),
                              bias, weights).reshape(n, D_MODEL)
            k_all, v_all, u_all = k3, v3, u3
        else:
            k_all = jnp.concatenate([caches[0][i].reshape(b, WINDOW, KV_WIDTH), k3], axis=1)
            v_all = jnp.concatenate([caches[1][i].reshape(b, WINDOW, KV_WIDTH), v3], axis=1)
            u_all = jnp.concatenate([caches[2][i], u3], axis=1)
            pad = ((0, 0), (0, KEY_WIN - WINDOW - t), (0, 0))
            k_win = jnp.pad(k_all, pad).reshape(b * KEY_WIN, KV_WIDTH)
            v_win = jnp.pad(v_all, pad).reshape(b * KEY_WIN, KV_WIDTH)
            u_hist = jnp.pad(caches[2][i], ((0, 0), (HIST_ROWS - POOL_HIST, 0), (0, 0)))
            h = _mixer_sample(cfg, sinks, q, k_win, v_win, u_hist, u, gates, h, bias, weights)
        ks.append(k_all[:, -WINDOW:].reshape(b, WINDOW, N_KV_HEADS, HEAD_DIM))
        vs.append(v_all[:, -WINDOW:].reshape(b, WINDOW, N_KV_HEADS, HEAD_DIM))
        us.append(u_all[:, -POOL_HIST:])

        j = i // 2
        tail = (vec(prm['g_ffn_post'][i]), (p.reshape(depth, n, PLE_DIM), i), prm['w_ple'][i], prm['w_ple_gate'][i])
        if i % 2 == 0:
            h = _ffn(h, vec(prm['g_ffn_pre'][i]), prm['w_gate_dense'][j], prm['w_up_dense'][j],
                     prm['w_down_dense'][j], *tail, _row_tile(n, 512))
        else:
            h = _moe(h, vec(prm['g_ffn_pre'][i]), prm['w_router'][j], prm['w_gate_moe'][j],
                     prm['w_up_moe'][j], prm['w_down_moe'][j], *tail)
    return h.reshape(b, t, D_MODEL), jnp.stack(ks), jnp.stack(vs), jnp.stack(us)


def kernel(x_prompt, x_sample, cache_k, cache_v, state_pool, p_prompt, p_sample, w_in, b_in, attn_sinks, w_attn_up, w_pool_group, pool_scale, w_pool_up, w_out, g_mix_pre, g_mix_post, g_ffn_pre, g_ffn_post, w_gate_dense, w_up_dense, w_down_dense, w_router, w_gate_moe, w_up_moe, w_down_moe, w_ple, w_ple_gate):
    prm = {
        'w_in': w_in, 'b_in': b_in, 'attn_sinks': attn_sinks, 'w_attn_up': w_attn_up,
        'w_pool_group': w_pool_group, 'pool_scale': pool_scale, 'w_pool_up': w_pool_up,
        'w_out': w_out, 'g_mix_pre': g_mix_pre, 'g_mix_post': g_mix_post,
        'g_ffn_pre': g_ffn_pre, 'g_ffn_post': g_ffn_post,
        'w_gate_dense': w_gate_dense, 'w_up_dense': w_up_dense, 'w_down_dense': w_down_dense,
        'w_router': w_router, 'w_gate_moe': w_gate_moe, 'w_up_moe': w_up_moe,
        'w_down_moe': w_down_moe, 'w_ple': w_ple, 'w_ple_gate': w_ple_gate,
    }
    prm = _prepare(prm)
    y_prompt, k_prompt, v_prompt, pool_prompt = _trunk(x_prompt, p_prompt, None, prm, 0)
    y_sample, k_sample, v_sample, pool_sample = _trunk(
        x_sample, p_sample, (cache_k, cache_v, state_pool), prm, PAST_LEN)
    return (y_prompt, y_sample, k_prompt, v_prompt, pool_prompt, k_sample, v_sample, pool_sample)
```

```python
import functools
from typing import NamedTuple

import jax
import jax.numpy as jnp
from jax import lax
from jax.experimental import pallas as pl
from jax.experimental.pallas import tpu as pltpu

F32 = jnp.float32
BF16 = jnp.bfloat16

D_MODEL = 1024
CHUNK = 64
WINDOW = 128
N_HEADS = 16
N_KV_HEADS = 2
GROUP = N_HEADS // N_KV_HEADS
HEAD_DIM = 64
ATTN_WIDTH = N_HEADS * HEAD_DIM
KV_WIDTH = N_KV_HEADS * HEAD_DIM
POOL_WINDOWS = (2, 4, 8, 16)
POOL_GROUP_DIM = 128
POOL_WIDTH = len(POOL_WINDOWS) * POOL_GROUP_DIM
POOL_HIST = max(POOL_WINDOWS) - 1
GATE_WIDTH = 2 * D_MODEL
IN_WIDTH = ATTN_WIDTH + 2 * KV_WIDTH + POOL_WIDTH + GATE_WIDTH
N_EXPERTS = 8
PLE_DIM = 256
PAST_LEN = 2048
EPS = 1e-6
NEG_INF = -1e30
LOG2E = 1.4426950408889634

LANES = 128
V7X_VMEM_BYTES = 64 * 1024 * 1024
VMEM_LIMIT = V7X_VMEM_BYTES * 7 // 8

HEAD_PAIR = 2 * HEAD_DIM
PAIRS_PER_KV = GROUP // 2
KEY_WIN = 2 * WINDOW
HIST_ROWS = POOL_HIST + 1
POOL_PAD = 8
COL_CHUNK = 512
FF_CHUNK_DENSE = 2048
FF_CHUNK_MOE = 1792
FF_SUB = 256


def _rms(x, g):
    return x * lax.rsqrt(jnp.mean(x * x, axis=-1, keepdims=True) + EPS) * g


def _dot(a, b):
    return jnp.dot(a, b, preferred_element_type=F32)


def _params(n_grid):
    return pltpu.CompilerParams(dimension_semantics=("arbitrary",) * n_grid,
                                vmem_limit_bytes=VMEM_LIMIT)


def _const_spec(shape):
    return pl.BlockSpec(shape, lambda *_: (0,) * len(shape))


def _in_proj_kernel(h_ref, g_ref, w_ref, b_ref, q_ref, k_ref, v_ref, u_ref, gate_ref, xn_ref):
    xn_ref[...] = _rms(h_ref[...], g_ref[...]).astype(BF16)

    def proj(lo, hi):
        return _dot(xn_ref[...], w_ref[:, lo:hi]) + b_ref[:, lo:hi]

    c1 = ATTN_WIDTH
    c2 = c1 + KV_WIDTH
    c3 = c2 + KV_WIDTH
    c4 = c3 + POOL_WIDTH
    for lo in range(0, c1, COL_CHUNK):
        q_ref[:, lo:lo + COL_CHUNK] = (proj(lo, lo + COL_CHUNK) * (HEAD_DIM ** -0.5 * LOG2E)).astype(BF16)
    kv = proj(c1, c3)
    k_ref[...] = kv[:, :KV_WIDTH]
    v_ref[...] = kv[:, KV_WIDTH:]
    u_ref[...] = proj(c3, c4)
    for lo in range(0, GATE_WIDTH, COL_CHUNK):
        gate_ref[:, lo:lo + COL_CHUNK] = jax.nn.sigmoid(proj(c4 + lo, c4 + lo + COL_CHUNK)).astype(BF16)


def _in_proj(h, g, w, b, tm):
    n = h.shape[0]
    row = lambda width: pl.BlockSpec((tm, width), lambda i: (i, 0))
    return pl.pallas_call(
        _in_proj_kernel,
        grid=(n // tm,),
        in_specs=[row(D_MODEL), _const_spec((1, D_MODEL)), _const_spec((D_MODEL, IN_WIDTH)),
                  _const_spec((1, IN_WIDTH))],
        out_specs=[row(ATTN_WIDTH), row(KV_WIDTH), row(KV_WIDTH), row(POOL_WIDTH), row(GATE_WIDTH)],
        out_shape=[jax.ShapeDtypeStruct((n, ATTN_WIDTH), BF16),
                   jax.ShapeDtypeStruct((n, KV_WIDTH), F32),
                   jax.ShapeDtypeStruct((n, KV_WIDTH), F32),
                   jax.ShapeDtypeStruct((n, POOL_WIDTH), F32),
                   jax.ShapeDtypeStruct((n, GATE_WIDTH), BF16)],
        scratch_shapes=[pltpu.VMEM((tm, D_MODEL), BF16)],
        compiler_params=_params(1),
    )(h, g, w, b)


class MixerCfg(NamedTuple):
    lq: int
    n_units: int
    win_stride: int
    n_seg: int
    seg_len: int
    has_halo: bool
    pos0: int

    @property
    def tb(self):
        return self.lq * self.n_units

    @property
    def key_rows(self):
        return (self.n_units - 1) * self.win_stride + KEY_WIN


def _fill_block_diag(dst_ref, x):
    lane = lax.broadcasted_iota(jnp.int32, x.shape, 1)
    lo = lane < HEAD_DIM
    xr = pltpu.roll(x, HEAD_DIM, 1)
    zero = jnp.zeros_like(x)
    dst_ref[0, 0] = jnp.where(lo, x, zero).astype(BF16)
    dst_ref[0, 1] = jnp.where(lo, zero, xr).astype(BF16)
    dst_ref[1, 0] = jnp.where(lo, xr, zero).astype(BF16)
    dst_ref[1, 1] = jnp.where(lo, zero, x).astype(BF16)


def _attention(cfg, q_ref, kbd_ref, vbd_ref, bias_ref, sink_ref, attn_ref, first_block):
    lq = cfg.lq

    for n in range(cfg.n_units):
        off_q = n * lq
        off_w = n * cfg.win_stride
        bias_idx = jnp.where(first_block, 1, 0) if (cfg.has_halo and n == 0) else 0
        for kv in range(N_KV_HEADS):
            cols = [LANES * (PAIRS_PER_KV * kv + p) for p in range(PAIRS_PER_KV)]
            q2 = jnp.concatenate([q_ref[pl.ds(off_q, lq), c:c + HEAD_PAIR] for c in cols], axis=0)
            kbd = jnp.concatenate([kbd_ref[kv, 0, pl.ds(off_w, KEY_WIN), :],
                                   kbd_ref[kv, 1, pl.ds(off_w, KEY_WIN), :]], axis=0)
            vbd = jnp.concatenate([vbd_ref[kv, 0, pl.ds(off_w, KEY_WIN), :],
                                   vbd_ref[kv, 1, pl.ds(off_w, KEY_WIN), :]], axis=0)
            s_all = lax.dot_general(q2, kbd, (((1,), (1,)), ((), ())), preferred_element_type=F32)
            probs, dens = [], []
            for p in range(PAIRS_PER_KV):
                pair_e = []
                for j in range(2):
                    head = kv * GROUP + 2 * p + j
                    s = s_all[p * lq:(p + 1) * lq, j * KEY_WIN:(j + 1) * KEY_WIN] + bias_ref[bias_idx, head]
                    sink = sink_ref[head] * LOG2E
                    m = jnp.maximum(jnp.max(s, axis=-1, keepdims=True), sink)
                    e = jnp.exp2(s - m)
                    dens.append(jnp.sum(e, axis=-1, keepdims=True) + jnp.exp2(sink - m))
                    pair_e.append(e.astype(BF16))
                probs.append(jnp.concatenate(pair_e, axis=1))
            o_all = _dot(jnp.concatenate(probs, axis=0), vbd)
            lane = lax.broadcasted_iota(jnp.int32, (lq, HEAD_PAIR), 1)
            for p in range(PAIRS_PER_KV):
                den = jnp.where(lane < HEAD_DIM, dens[2 * p], dens[2 * p + 1])
                o = o_all[p * lq:(p + 1) * lq] / den
                attn_ref[pl.ds(off_q, lq), cols[p]:cols[p] + HEAD_PAIR] = o.astype(BF16)


def _pooling(cfg, u_ref, ext_ref, tmp_ref, wpg_ref, scale_ref, pool_ref, pos_base):
    assert POOL_WINDOWS == (2, 4, 8, 16)
    ln = cfg.seg_len
    r0 = POOL_PAD + HIST_ROWS
    tr = r0 + ln
    row = lax.broadcasted_iota(jnp.int32, (ln, 1), 0)
    gd = POOL_GROUP_DIM
    for seg in range(cfg.n_seg):
        a, b = ext_ref.at[seg], tmp_ref.at[seg]

        def doubled(src, dst, shift, lo):
            dst[POOL_PAD:tr, lo:] = src[POOL_PAD:tr, lo:] + src[POOL_PAD - shift:tr - shift, lo:]

        b[0:POOL_PAD, :] = jnp.zeros((POOL_PAD, POOL_WIDTH), F32)
        doubled(a, b, 1, 0)
        doubled(b, a, 2, gd)
        doubled(a, b, 4, 2 * gd)
        sums = (b[r0:tr, 0:gd], a[r0:tr, gd:2 * gd], b[r0:tr, 2 * gd:3 * gd],
                b[r0:tr, 3 * gd:] + b[r0 - 8:tr - 8, 3 * gd:])
        pos = pos_base + row
        for g, w in enumerate(POOL_WINDOWS):
            lanes = slice(g * gd, (g + 1) * gd)
            u = u_ref[seg * ln:(seg + 1) * ln, lanes]
            cnt = jnp.minimum(pos + 1, w).astype(F32)
            mixed = (sums[g] / cnt - u).astype(BF16)
            y = _dot(mixed, wpg_ref[g]) * scale_ref[:, lanes]
            pool_ref[seg * ln:(seg + 1) * ln, lanes] = y.astype(BF16)


def _mixer_kernel(cfg, *refs):
    if cfg.has_halo:
        (sink_ref, q_ref, kh_ref, k_ref, vh_ref, v_ref, uh_ref, u_ref, gate_ref, h_ref, bias_ref,
         wau_ref, wpg_ref, scale_ref, wpu_ref, wout_ref, gpost_ref, out_ref,
         kbd_ref, vbd_ref, ext_ref, tmp_ref, attn_ref, pool_ref) = refs
    else:
        (sink_ref, q_ref, k_ref, v_ref, uh_ref, u_ref, gate_ref, h_ref, bias_ref,
         wau_ref, wpg_ref, scale_ref, wpu_ref, wout_ref, gpost_ref, out_ref,
         kbd_ref, vbd_ref, ext_ref, tmp_ref, attn_ref, pool_ref) = refs

    step = pl.program_id(1) if cfg.has_halo else pl.program_id(0)
    first_block = step == 0

    if cfg.has_halo:
        _fill_block_diag(kbd_ref, jnp.concatenate([kh_ref[...], k_ref[...]], axis=0))
        _fill_block_diag(vbd_ref, jnp.concatenate([vh_ref[...], v_ref[...]], axis=0))
    else:
        _fill_block_diag(kbd_ref, k_ref[...])
        _fill_block_diag(vbd_ref, v_ref[...])
    _attention(cfg, q_ref, kbd_ref, vbd_ref, bias_ref, sink_ref, attn_ref, first_block)

    r0 = POOL_PAD + HIST_ROWS
    zero_pad = jnp.zeros((POOL_PAD, POOL_WIDTH), F32)
    if cfg.has_halo:
        hist = uh_ref[...]
        ext_ref[0, 0:POOL_PAD, :] = zero_pad
        ext_ref[0, POOL_PAD:r0, :] = jnp.where(first_block, jnp.zeros_like(hist), hist)
        ext_ref[0, r0:, :] = u_ref[...]
        pos_base = cfg.pos0 + step * cfg.tb
    else:
        for seg in range(cfg.n_seg):
            ext_ref[seg, 0:POOL_PAD, :] = zero_pad
            ext_ref[seg, POOL_PAD:r0, :] = uh_ref[seg]
            ext_ref[seg, r0:, :] = u_ref[seg * cfg.seg_len:(seg + 1) * cfg.seg_len, :]
        pos_base = cfg.pos0
    _pooling(cfg, u_ref, ext_ref, tmp_ref, wpg_ref, scale_ref, pool_ref, pos_base)

    a = _dot(attn_ref[...], wau_ref[...])
    p = _dot(pool_ref[...], wpu_ref[...])
    merged = gate_ref[:, :D_MODEL].astype(F32) * a + gate_ref[:, D_MODEL:].astype(F32) * p
    o = _dot(merged.astype(BF16), wout_ref[...])
    out_ref[...] = h_ref[...] + _rms(o, gpost_ref[...])


def _mixer_weights_specs():
    return [_const_spec((ATTN_WIDTH, D_MODEL)),
            _const_spec((len(POOL_WINDOWS), POOL_GROUP_DIM, POOL_GROUP_DIM)),
            _const_spec((1, POOL_WIDTH)),
            _const_spec((POOL_WIDTH, D_MODEL)),
            _const_spec((D_MODEL, D_MODEL)),
            _const_spec((1, D_MODEL))]


def _mixer_scratch(cfg):
    return [pltpu.VMEM((N_KV_HEADS, 2, cfg.key_rows, KV_WIDTH), BF16),
            pltpu.VMEM((N_KV_HEADS, 2, cfg.key_rows, KV_WIDTH), BF16),
            pltpu.VMEM((cfg.n_seg, POOL_PAD + HIST_ROWS + cfg.seg_len, POOL_WIDTH), F32),
            pltpu.VMEM((cfg.n_seg, POOL_PAD + HIST_ROWS + cfg.seg_len, POOL_WIDTH), F32),
            pltpu.VMEM((cfg.tb, ATTN_WIDTH), BF16),
            pltpu.VMEM((cfg.tb, POOL_WIDTH), BF16)]


def _mixer_prompt(cfg, sinks, q, k, v, u, gates, h, bias, weights):
    b, t = q.shape[:2]
    tb = cfg.tb
    halo_k = tb // WINDOW
    halo_u = tb // HIST_ROWS
    cur = lambda width: pl.BlockSpec((None, tb, width), lambda bi, i: (bi, i, 0))
    prev_k = pl.BlockSpec((None, WINDOW, KV_WIDTH), lambda bi, i: (bi, jnp.maximum(i * halo_k - 1, 0), 0))
    prev_u = pl.BlockSpec((None, HIST_ROWS, POOL_WIDTH), lambda bi, i: (bi, jnp.maximum(i * halo_u - 1, 0), 0))
    return pl.pallas_call(
        functools.partial(_mixer_kernel, cfg),
        grid=(b, t // tb),
        in_specs=[pl.BlockSpec(memory_space=pltpu.SMEM),
                  cur(ATTN_WIDTH), prev_k, cur(KV_WIDTH), prev_k, cur(KV_WIDTH),
                  prev_u, cur(POOL_WIDTH), cur(GATE_WIDTH), cur(D_MODEL),
                  _const_spec(bias.shape)] + _mixer_weights_specs(),
        out_specs=cur(D_MODEL),
        out_shape=jax.ShapeDtypeStruct((b, t, D_MODEL), F32),
        scratch_shapes=_mixer_scratch(cfg),
        compiler_params=_params(2),
    )(sinks, q, k, k, v, v, u, u, gates, h, bias, *weights)


def _mixer_sample(cfg, sinks, q, k_win, v_win, u_hist, u, gates, h, bias, weights):
    n = q.shape[0]
    tb = cfg.tb
    row = lambda width: pl.BlockSpec((tb, width), lambda i: (i, 0))
    return pl.pallas_call(
        functools.partial(_mixer_kernel, cfg),
        grid=(n // tb,),
        in_specs=[pl.BlockSpec(memory_space=pltpu.SMEM),
                  row(ATTN_WIDTH),
                  pl.BlockSpec((cfg.key_rows, KV_WIDTH), lambda i: (i, 0)),
                  pl.BlockSpec((cfg.key_rows, KV_WIDTH), lambda i: (i, 0)),
                  pl.BlockSpec((cfg.n_seg, HIST_ROWS, POOL_WIDTH), lambda i: (i, 0, 0)),
                  row(POOL_WIDTH), row(GATE_WIDTH), row(D_MODEL),
                  _const_spec(bias.shape)] + _mixer_weights_specs(),
        out_specs=row(D_MODEL),
        out_shape=jax.ShapeDtypeStruct((n, D_MODEL), F32),
        scratch_shapes=_mixer_scratch(cfg),
        compiler_params=_params(1),
    )(sinks, q, k_win, v_win, u_hist, u, gates, h, bias, *weights)


def _alibi_bias(lq, q_off, valid_fn):
    slopes = 2.0 ** (-8.0 * jnp.arange(1, N_HEADS + 1, dtype=F32) / N_HEADS)
    r = jnp.arange(lq)[:, None]
    j = jnp.arange(KEY_WIN)[None, :]
    dist = jnp.abs(r + q_off - j).astype(F32)
    tables = []
    for valid in valid_fn(r, j):
        tables.append(jnp.where(valid[None], -(slopes[:, None, None] * dist[None]) * LOG2E, NEG_INF))
    return jnp.stack(tables)


def _swiglu_chunk(x, wg_ref, wu_ref, wd_ref):
    fc = wg_ref.shape[1]
    spans = [(s, min(s + FF_SUB, fc)) for s in range(0, fc, FF_SUB)]
    gate_up = [(_dot(x, wg_ref[:, a:b]), _dot(x, wu_ref[:, a:b])) for a, b in spans]
    y = None
    for (g, u), (a, b) in zip(gate_up, spans):
        part = _dot((jax.nn.silu(g) * u).astype(BF16), wd_ref[a:b, :])
        y = part if y is None else y + part
    return y


def _ffn_epilogue(h, f, gpost_ref, emb, wpg_ref):
    h2 = h + _rms(f, gpost_ref[...])
    return h2 + emb * jax.nn.sigmoid(_dot(h2.astype(BF16), wpg_ref[...]))


def _ffn_kernel(h_ref, gpre_ref, wg_ref, wu_ref, wd_ref, gpost_ref, p_ref, wple_ref, wpg_ref,
                out_ref, hn_ref, acc_ref):
    f = pl.program_id(1)

    @pl.when(f == 0)
    def _():
        hn_ref[...] = _rms(h_ref[...], gpre_ref[...]).astype(BF16)
        acc_ref[...] = _swiglu_chunk(hn_ref[...], wg_ref, wu_ref, wd_ref)

    @pl.when(f > 0)
    def _():
        acc_ref[...] += _swiglu_chunk(hn_ref[...], wg_ref, wu_ref, wd_ref)

    @pl.when(f == pl.num_programs(1) - 1)
    def _():
        emb = _dot(p_ref[...].astype(BF16), wple_ref[...])
        out_ref[...] = _ffn_epilogue(h_ref[...], acc_ref[...], gpost_ref, emb, wpg_ref)


def _ffn(h, gpre, wg, wu, wd, gpost, p, w_ple, w_ple_gate, tm):
    n = h.shape[0]
    fc = wg.shape[2]
    row = lambda width: pl.BlockSpec((tm, width), lambda i, f: (i, 0))
    const = lambda shape: pl.BlockSpec(shape, lambda i, f: (0,) * len(shape))
    return pl.pallas_call(
        _ffn_kernel,
        grid=(n // tm, wg.shape[0]),
        in_specs=[row(D_MODEL), const((1, D_MODEL)),
                  pl.BlockSpec((None, D_MODEL, fc), lambda i, f: (f, 0, 0)),
                  pl.BlockSpec((None, D_MODEL, fc), lambda i, f: (f, 0, 0)),
                  pl.BlockSpec((fc, D_MODEL), lambda i, f: (f, 0)),
                  const((1, D_MODEL)), pl.BlockSpec((None, tm, PLE_DIM), lambda i, f: (p[1], i, 0)),
                  const((PLE_DIM, D_MODEL)), const((D_MODEL, D_MODEL))],
        out_specs=row(D_MODEL),
        out_shape=jax.ShapeDtypeStruct((n, D_MODEL), F32),
        scratch_shapes=[pltpu.VMEM((tm, D_MODEL), BF16), pltpu.VMEM((tm, D_MODEL), F32)],
        compiler_params=_params(2),
    )(h, gpre, wg, wu, wd, gpost, p[0], w_ple, w_ple_gate)


SUBLANES = 8
COPY_ROWS = 32
ROUTE_TILE = 512
EXPERT_ROWS = 1024
EXPERT_ROWS_SMALL = 256
ZERO_ROWS = 128
REC_E1, REC_E2, REC_POS1, REC_POS2, REC_W1, REC_W2 = range(6)


def _sorted_rows(tm):
    return 2 * tm + LANES


def _route_kernel(h_ref, gpre_ref, wr_ref, hn_ref, rec_ref, cnt_ref):
    tm = h_ref.shape[0]
    hn = _rms(h_ref[...], gpre_ref[...]).astype(BF16)
    hn_ref[...] = hn
    logits = _dot(hn, wr_ref[...])
    lane = lax.broadcasted_iota(jnp.int32, logits.shape, 1).astype(F32)
    logits = jnp.where(lane < N_EXPERTS, logits, -jnp.inf)
    m1 = jnp.max(logits, axis=-1, keepdims=True)
    i1 = jnp.min(jnp.where(logits == m1, lane, float(LANES)), axis=-1, keepdims=True)
    rest = jnp.where(lane == i1, -jnp.inf, logits)
    m2 = jnp.max(rest, axis=-1, keepdims=True)
    i2 = jnp.min(jnp.where(rest == m2, lane, float(LANES)), axis=-1, keepdims=True)
    t = jnp.exp(m2 - m1)
    den = 1.0 + t

    sel = jnp.where(jnp.logical_or(lane == i1, lane == i2), 1.0, 0.0)
    r = lax.broadcasted_iota(jnp.int32, (tm, tm), 0)
    c = lax.broadcasted_iota(jnp.int32, (tm, tm), 1)
    rank = _dot(jnp.where(c < r, 1.0, 0.0).astype(BF16), sel.astype(BF16))
    cnt = jnp.sum(sel, axis=0, keepdims=True)
    padded = jnp.ceil(cnt * (1.0 / SUBLANES)) * SUBLANES
    er = lax.broadcasted_iota(jnp.int32, (LANES, LANES), 0)
    ec = lax.broadcasted_iota(jnp.int32, (LANES, LANES), 1)
    seg_start = _dot(jnp.broadcast_to(padded, (SUBLANES, LANES)).astype(BF16),
                     jnp.where(er < ec, 1.0, 0.0).astype(BF16))[0:1]
    pos = seg_start + rank
    pos1 = jnp.sum(jnp.where(lane == i1, pos, 0.0), axis=-1, keepdims=True)
    pos2 = jnp.sum(jnp.where(lane == i2, pos, 0.0), axis=-1, keepdims=True)
    rec = jnp.zeros_like(logits)
    for idx, val in ((REC_E1, i1), (REC_E2, i2), (REC_POS1, pos1), (REC_POS2, pos2),
                     (REC_W1, 1.0 / den), (REC_W2, t / den)):
        rec = jnp.where(lane == idx, val, rec)
    rec_ref[...] = rec
    cnt_ref[...] = cnt


def _route(h, gpre, w_router, tm):
    n = h.shape[0]
    row = lambda width: pl.BlockSpec((tm, width), lambda i: (i, 0))
    return pl.pallas_call(
        _route_kernel,
        grid=(n // tm,),
        in_specs=[row(D_MODEL), _const_spec((1, D_MODEL)), _const_spec((D_MODEL, LANES))],
        out_specs=[row(D_MODEL), row(LANES), pl.BlockSpec((None, 1, LANES), lambda i: (i, 0, 0))],
        out_shape=[jax.ShapeDtypeStruct((n, D_MODEL), BF16),
                   jax.ShapeDtypeStruct((n, LANES), F32),
                   jax.ShapeDtypeStruct((n // tm, 1, LANES), F32)],
        compiler_params=_params(1),
    )(h, gpre, w_router)


def _group_copy(src_ref, dst_ref, src_row, dst_row, sem, rows=SUBLANES):
    return pltpu.make_async_copy(src_ref.at[pl.ds(src_row, rows)], dst_ref.at[pl.ds(dst_row, rows)], sem)


def _for_each_group(tile, pc_ref, lb_ref, base_ref, fn):
    for e in range(N_EXPERTS):
        idx = tile * N_EXPERTS + e
        local0 = lb_ref[idx]
        global0 = base_ref[idx]
        n_big = pc_ref[idx] // COPY_ROWS
        done = n_big * COPY_ROWS

        def big(g, carry, local0=local0, global0=global0):
            fn(pl.multiple_of(local0 + g * COPY_ROWS, SUBLANES), pl.multiple_of(global0 + g * COPY_ROWS, SUBLANES),
               COPY_ROWS)
            return carry

        def small(g, carry, local0=local0 + done, global0=global0 + done):
            fn(pl.multiple_of(local0 + g * SUBLANES, SUBLANES), pl.multiple_of(global0 + g * SUBLANES, SUBLANES),
               SUBLANES)
            return carry

        lax.fori_loop(0, n_big, big, 0)
        lax.fori_loop(0, (pc_ref[idx] - done) // SUBLANES, small, 0)


def _tile_groups(tile, pc_ref, lb_ref):
    last = tile * N_EXPERTS + N_EXPERTS - 1
    return (lb_ref[last] + pc_ref[last]) // SUBLANES


def _wait_groups(n_groups, src_ref, dst_ref, sem):
    @pl.when(n_groups > 0)
    def _():
        rows = n_groups * SUBLANES
        pltpu.make_async_copy(src_ref.at[pl.ds(0, rows)], dst_ref.at[pl.ds(0, rows)], sem).wait()


def _zero_copy(zero_ref, xs_ref, row, n_rows, sem):
    return pltpu.make_async_copy(zero_ref.at[pl.ds(0, n_rows)], xs_ref.at[pl.ds(row, n_rows)], sem)


def _dispatch_kernel(pc_ref, lb_ref, base_ref, tail_ref, hn_ref, rec_ref, xs_ref, tile_ref, zero_ref, sem, zsem):
    tile = pl.program_id(0)
    n_tiles = pl.num_programs(0)
    slot = tile % 2
    tm = hn_ref.shape[0]
    rec_t = rec_ref[...].T
    row = lax.broadcasted_iota(jnp.int32, (_sorted_rows(tm), tm), 0).astype(F32)
    hit = jnp.logical_or(row == rec_t[REC_POS1:REC_POS1 + 1], row == rec_t[REC_POS2:REC_POS2 + 1])
    tile_ref[slot] = _dot(jnp.where(hit, 1.0, 0.0).astype(BF16), hn_ref[...])
    _for_each_group(tile, pc_ref, lb_ref, base_ref,
                    lambda lo, gl, rows: _group_copy(tile_ref.at[slot], xs_ref, lo, gl, sem.at[slot], rows).start())

    @pl.when(tile > 0)
    def _():
        _wait_groups(_tile_groups(tile - 1, pc_ref, lb_ref), tile_ref.at[1 - slot], xs_ref, sem.at[1 - slot])

    @pl.when(tile == n_tiles - 1)
    def _():
        _wait_groups(_tile_groups(tile, pc_ref, lb_ref), tile_ref.at[slot], xs_ref, sem.at[slot])
        zero_ref[...] = jnp.zeros_like(zero_ref)
        for e in range(N_EXPERTS):
            start = tail_ref[e]
            lax.fori_loop(0, tail_ref[N_EXPERTS + e], lambda g, c, start=start: (_zero_copy(
                zero_ref, xs_ref, pl.multiple_of(start + g * SUBLANES, SUBLANES), SUBLANES, zsem.at[0]).start(), c)[1], 0)
            lax.fori_loop(0, tail_ref[N_EXPERTS + e], lambda g, c: (_zero_copy(
                zero_ref, xs_ref, 0, SUBLANES, zsem.at[0]).wait(), c)[1], 0)
        used_rows = tail_ref[2 * N_EXPERTS]
        n_blocks = (xs_ref.shape[0] - used_rows) // ZERO_ROWS
        lax.fori_loop(0, n_blocks, lambda g, c: (_zero_copy(
            zero_ref, xs_ref, pl.multiple_of(used_rows + g * ZERO_ROWS, ZERO_ROWS), ZERO_ROWS, zsem.at[1]).start(), c)[1], 0)
        lax.fori_loop(0, n_blocks, lambda g, c: (_zero_copy(
            zero_ref, xs_ref, 0, ZERO_ROWS, zsem.at[1]).wait(), c)[1], 0)


def _dispatch(pc, lb, base, tail, hn, rec, max_rows, tm):
    n = hn.shape[0]
    row = lambda width: pl.BlockSpec((tm, width), lambda i, *_: (i, 0))
    return pl.pallas_call(
        _dispatch_kernel,
        grid_spec=pltpu.PrefetchScalarGridSpec(
            num_scalar_prefetch=4,
            grid=(n // tm,),
            in_specs=[row(D_MODEL), row(LANES)],
            out_specs=pl.BlockSpec(memory_space=pl.ANY),
            scratch_shapes=[pltpu.VMEM((2, _sorted_rows(tm), D_MODEL), F32),
                            pltpu.VMEM((ZERO_ROWS, D_MODEL), F32),
                            pltpu.SemaphoreType.DMA((2,)), pltpu.SemaphoreType.DMA((2,))]),
        out_shape=jax.ShapeDtypeStruct((max_rows, D_MODEL), F32),
        compiler_params=_params(1),
    )(pc, lb, base, tail, hn, rec)


def _experts_kernel(eid_ref, nused_ref, x_ref, wg_ref, wu_ref, wd_ref, y_ref, xb_ref):
    del eid_ref
    f = pl.program_id(1)
    used = pl.program_id(0) < nused_ref[0]

    @pl.when(jnp.logical_and(used, f == 0))
    def _():
        xb_ref[...] = x_ref[...].astype(BF16)
        y_ref[...] = _swiglu_chunk(xb_ref[...], wg_ref, wu_ref, wd_ref)

    @pl.when(jnp.logical_and(used, f > 0))
    def _():
        y_ref[...] += _swiglu_chunk(xb_ref[...], wg_ref, wu_ref, wd_ref)

    @pl.when(jnp.logical_and(jnp.logical_not(used), f == 0))
    def _():
        y_ref[...] = jnp.zeros_like(y_ref)


def _experts(eid, n_used, xs, wg, wu, wd, rt):
    rows = xs.shape[0]
    n_f, fc = wg.shape[1], wg.shape[3]
    tile = lambda r, nu: jnp.minimum(r, nu[0] - 1)
    chunk = lambda r, f, nu: jnp.where(r < nu[0], f, n_f - 1)
    return pl.pallas_call(
        _experts_kernel,
        grid_spec=pltpu.PrefetchScalarGridSpec(
            num_scalar_prefetch=2,
            grid=(rows // rt, n_f),
            in_specs=[pl.BlockSpec((rt, D_MODEL), lambda r, f, eid, nu: (tile(r, nu), 0)),
                      pl.BlockSpec((None, None, D_MODEL, fc),
                                   lambda r, f, eid, nu: (eid[tile(r, nu)], chunk(r, f, nu), 0, 0)),
                      pl.BlockSpec((None, None, D_MODEL, fc),
                                   lambda r, f, eid, nu: (eid[tile(r, nu)], chunk(r, f, nu), 0, 0)),
                      pl.BlockSpec((None, fc, D_MODEL), lambda r, f, eid, nu: (eid[tile(r, nu)], chunk(r, f, nu), 0))],
            out_specs=pl.BlockSpec((rt, D_MODEL), lambda r, f, eid, nu: (r, 0)),
            scratch_shapes=[pltpu.VMEM((rt, D_MODEL), BF16)]),
        out_shape=jax.ShapeDtypeStruct((rows, D_MODEL), F32),
        compiler_params=_params(2),
    )(eid, n_used, xs, wg, wu, wd)


def _combine_kernel(pc_ref, lb_ref, base_ref, rec_ref, h_ref, gpost_ref, p_ref, wple_ref, wpg_ref, ys_ref,
                    out_ref, tile_ref, sem):
    tile = pl.program_id(0)
    slot = tile % 2
    tm = h_ref.shape[0]
    rows = _sorted_rows(tm)

    def fetch(t, s):
        _for_each_group(t, pc_ref, lb_ref, base_ref,
                        lambda lo, gl, rows: _group_copy(ys_ref, tile_ref.at[s], gl, lo, sem.at[s], rows).start())

    @pl.when(tile == 0)
    def _():
        fetch(tile, slot)

    @pl.when(tile + 1 < pl.num_programs(0))
    def _():
        fetch(tile + 1, 1 - slot)

    rec = rec_ref[...]
    col = lax.broadcasted_iota(jnp.int32, (tm, rows), 1).astype(F32)
    weights = (jnp.where(col == rec[:, REC_POS1:REC_POS1 + 1], rec[:, REC_W1:REC_W1 + 1], 0.0)
               + jnp.where(col == rec[:, REC_POS2:REC_POS2 + 1], rec[:, REC_W2:REC_W2 + 1], 0.0))
    emb = _dot(p_ref[...].astype(BF16), wple_ref[...])
    n_groups = _tile_groups(tile, pc_ref, lb_ref)
    _wait_groups(n_groups, ys_ref, tile_ref.at[slot], sem.at[slot])
    live = lax.broadcasted_iota(jnp.int32, (rows, 1), 0) < n_groups * SUBLANES
    y = jnp.where(live, tile_ref[slot], 0.0).astype(BF16)
    f = _dot(weights.astype(BF16), y)
    out_ref[...] = _ffn_epilogue(h_ref[...], f, gpost_ref, emb, wpg_ref)


def _combine(pc, lb, base, rec, h, gpost, p, w_ple, w_ple_gate, ys, tm):
    n = h.shape[0]
    row = lambda width: pl.BlockSpec((tm, width), lambda i, *_: (i, 0))
    const = lambda shape: pl.BlockSpec(shape, lambda i, *_: (0,) * len(shape))
    return pl.pallas_call(
        _combine_kernel,
        grid_spec=pltpu.PrefetchScalarGridSpec(
            num_scalar_prefetch=3,
            grid=(n // tm,),
            in_specs=[row(LANES), row(D_MODEL), const((1, D_MODEL)),
                      pl.BlockSpec((None, tm, PLE_DIM), lambda i, *_: (p[1], i, 0)), const((PLE_DIM, D_MODEL)),
                      const((D_MODEL, D_MODEL)), pl.BlockSpec(memory_space=pl.ANY)],
            out_specs=row(D_MODEL),
            scratch_shapes=[pltpu.VMEM((2, _sorted_rows(tm), D_MODEL), F32), pltpu.SemaphoreType.DMA((2,))]),
        out_shape=jax.ShapeDtypeStruct((n, D_MODEL), F32),
        compiler_params=_params(1),
    )(pc, lb, base, rec, h, gpost, p[0], w_ple, w_ple_gate, ys)


def _moe(h, gpre, w_router, wg, wu, wd, gpost, p, w_ple, w_ple_gate):
    n = h.shape[0]
    tm = _row_tile(n, ROUTE_TILE)
    n_tiles = n // tm
    hn, rec, cnt = _route(h, gpre, w_router, tm)
    rt = EXPERT_ROWS if 2 * n >= 4 * N_EXPERTS * EXPERT_ROWS else EXPERT_ROWS_SMALL

    cnt = cnt[:, 0, :N_EXPERTS].astype(jnp.int32)
    pc = (cnt + SUBLANES - 1) // SUBLANES * SUBLANES
    lb = jnp.cumsum(pc, axis=1) - pc
    region = (jnp.sum(pc, axis=0) + rt - 1) // rt * rt
    region_end = jnp.cumsum(region)
    base = (region_end - region)[None, :] + jnp.cumsum(pc, axis=0) - pc
    max_rows = 2 * n + n_tiles * N_EXPERTS * (SUBLANES - 1) + N_EXPERTS * (rt - 1)
    max_rows = (max_rows + rt - 1) // rt * rt
    tile_start = jnp.arange(max_rows // rt, dtype=jnp.int32) * rt
    eid = jnp.minimum(jnp.sum(tile_start[:, None] >= region_end[None, :], axis=1), N_EXPERTS - 1).astype(jnp.int32)
    n_used = (region_end[-1:] // rt).astype(jnp.int32)
    total = jnp.sum(pc, axis=0)
    tail = jnp.concatenate([region_end - region + total, (region - total) // SUBLANES, region_end[-1:]])
    flat = lambda a: a.reshape(-1).astype(jnp.int32)

    xs = _dispatch(flat(pc), flat(lb), flat(base), flat(tail), hn, rec, max_rows, tm)
    ys = _experts(eid, n_used, xs, wg, wu, wd, rt)
    return _combine(flat(pc), flat(lb), flat(base), rec, h, gpost, p, w_ple, w_ple_gate, ys, tm)


def _row_tile(n, target):
    tm = min(n, target)
    assert n % tm == 0, (n, tm)
    return tm


def _column_chunks_kernel(w_ref, o_ref):
    o_ref[...] = w_ref[...].astype(BF16)


def _column_chunks(w, fc):
    e, d, ff = w.shape
    assert ff % fc == 0 and fc % LANES == 0
    return pl.pallas_call(
        _column_chunks_kernel,
        grid=(e, ff // fc),
        in_specs=[pl.BlockSpec((None, d, fc), lambda i, f: (i, 0, f))],
        out_specs=pl.BlockSpec((None, None, d, fc), lambda i, f: (i, f, 0, 0)),
        out_shape=jax.ShapeDtypeStruct((e, ff // fc, d, fc), BF16),
        compiler_params=_params(2),
    )(w)


def _prepare(prm):
    out = dict(prm)
    for name in ('w_in', 'w_attn_up', 'w_pool_group', 'w_pool_up', 'w_out', 'w_down_dense', 'w_down_moe',
                 'w_ple', 'w_ple_gate'):
        out[name] = prm[name].astype(BF16)
    out['w_router'] = jnp.pad(prm['w_router'], ((0, 0), (0, 0), (0, LANES - N_EXPERTS))).astype(BF16)
    out['w_gate_dense'] = _column_chunks(prm['w_gate_dense'], FF_CHUNK_DENSE)
    out['w_up_dense'] = _column_chunks(prm['w_up_dense'], FF_CHUNK_DENSE)
    n_moe, n_exp = prm['w_gate_moe'].shape[:2]
    for name in ('w_gate_moe', 'w_up_moe'):
        w = _column_chunks(prm[name].reshape((n_moe * n_exp,) + prm[name].shape[2:]), FF_CHUNK_MOE)
        out[name] = w.reshape((n_moe, n_exp) + w.shape[1:])
    return out


def _trunk(x, p, caches, prm, pos0):
    b, t, _ = x.shape
    n = b * t
    depth = prm['w_in'].shape[0]
    h = x.reshape(n, D_MODEL)
    vec = lambda a: a.reshape(1, -1)
    ks, vs, us = [], [], []

    if caches is None:
        cfg = MixerCfg(lq=2 * CHUNK, n_units=4, win_stride=WINDOW, n_seg=1, seg_len=8 * CHUNK,
                       has_halo=True, pos0=pos0)
        assert t % cfg.tb == 0
        bias = _alibi_bias(cfg.lq, WINDOW, lambda r, j: (
            (j // CHUNK >= r // CHUNK) & (j // CHUNK <= r // CHUNK + 2),
            (j // CHUNK >= r // CHUNK) & (j // CHUNK <= r // CHUNK + 2) & (j >= WINDOW)))
    else:
        seqs = 8
        cfg = MixerCfg(lq=t, n_units=seqs, win_stride=KEY_WIN, n_seg=seqs, seg_len=t,
                       has_halo=False, pos0=pos0)
        assert b % seqs == 0 and WINDOW + t <= KEY_WIN
        bias = _alibi_bias(cfg.lq, WINDOW, lambda r, j: ((j < WINDOW + t) & (r >= 0),))

    for i in range(depth):
        q, k, v, u, gates = _in_proj(h, vec(prm['g_mix_pre'][i]), prm['w_in'][i],
                                     vec(prm['b_in'][i]), _row_tile(n, 1024))
        weights = (prm['w_attn_up'][i], prm['w_pool_group'][i], vec(prm['pool_scale'][i]),
                   prm['w_pool_up'][i], prm['w_out'][i], vec(prm['g_mix_post'][i]))
        sinks = prm['attn_sinks'][i]
        k3 = k.reshape(b, t, KV_WIDTH)
        v3 = v.reshape(b, t, KV_WIDTH)
        u3 = u.reshape(b, t, POOL_WIDTH)
        if caches is None:
            h = _mixer_prompt(cfg, sinks, q.reshape(b, t, ATTN_WIDTH), k3, v3, u3,
                              gates.reshape(b, t, GATE_WIDTH), h.reshape(b, t, D_MODEL),
                              bias, weights).reshape(n, D_MODEL)
            k_all, v_all, u_all = k3, v3, u3
        else:
            k_all = jnp.concatenate([caches[0][i].reshape(b, WINDOW, KV_WIDTH), k3], axis=1)
            v_all = jnp.concatenate([caches[1][i].reshape(b, WINDOW, KV_WIDTH), v3], axis=1)
            u_all = jnp.concatenate([caches[2][i], u3], axis=1)
            pad = ((0, 0), (0, KEY_WIN - WINDOW - t), (0, 0))
            k_win = jnp.pad(k_all, pad).reshape(b * KEY_WIN, KV_WIDTH)
            v_win = jnp.pad(v_all, pad).reshape(b * KEY_WIN, KV_WIDTH)
            u_hist = jnp.pad(caches[2][i], ((0, 0), (HIST_ROWS - POOL_HIST, 0), (0, 0)))
            h = _mixer_sample(cfg, sinks, q, k_win, v_win, u_hist, u, gates, h, bias, weights)
        ks.append(k_all[:, -WINDOW:].reshape(b, WINDOW, N_KV_HEADS, HEAD_DIM))
        vs.append(v_all[:, -WINDOW:].reshape(b, WINDOW, N_KV_HEADS, HEAD_DIM))
        us.append(u_all[:, -POOL_HIST:])

        j = i // 2
        tail = (vec(prm['g_ffn_post'][i]), (p.reshape(depth, n, PLE_DIM), i), prm['w_ple'][i], prm['w_ple_gate'][i])
        if i % 2 == 0:
            h = _ffn(h, vec(prm['g_ffn_pre'][i]), prm['w_gate_dense'][j], prm['w_up_dense'][j],
                     prm['w_down_dense'][j], *tail, _row_tile(n, 512))
        else:
            h = _moe(h, vec(prm['g_ffn_pre'][i]), prm['w_router'][j], prm['w_gate_moe'][j],
                     prm['w_up_moe'][j], prm['w_down_moe'][j], *tail)
    return h.reshape(b, t, D_MODEL), jnp.stack(ks), jnp.stack(vs), jnp.stack(us)


def kernel(x_prompt, x_sample, cache_k, cache_v, state_pool, p_prompt, p_sample, w_in, b_in, attn_sinks, w_attn_up, w_pool_group, pool_scale, w_pool_up, w_out, g_mix_pre, g_mix_post, g_ffn_pre, g_ffn_post, w_gate_dense, w_up_dense, w_down_dense, w_router, w_gate_moe, w_up_moe, w_down_moe, w_ple, w_ple_gate):
    prm = {
        'w_in': w_in, 'b_in': b_in, 'attn_sinks': attn_sinks, 'w_attn_up': w_attn_up,
        'w_pool_group': w_pool_group, 'pool_scale': pool_scale, 'w_pool_up': w_pool_up,
        'w_out': w_out, 'g_mix_pre': g_mix_pre, 'g_mix_post': g_mix_post,
        'g_ffn_pre': g_ffn_pre, 'g_ffn_post': g_ffn_post,
        'w_gate_dense': w_gate_dense, 'w_up_dense': w_up_dense, 'w_down_dense': w_down_dense,
        'w_router': w_router, 'w_gate_moe': w_gate_moe, 'w_up_moe': w_up_moe,
        'w_down_moe': w_down_moe, 'w_ple': w_ple, 'w_ple_gate': w_ple_gate,
    }
    prm = _prepare(prm)
    y_prompt, k_prompt, v_prompt, pool_prompt = _trunk(x_prompt, p_prompt, None, prm, 0)
    y_sample, k_sample, v_sample, pool_sample = _trunk(
        x_sample, p_sample, (cache_k, cache_v, state_pool), prm, PAST_LEN)
    return (y_prompt, y_sample, k_prompt, v_prompt, pool_prompt, k_sample, v_sample, pool_sample)
```
